```python
import jax, jax.numpy as jnp
from jax import lax
import numpy as np

D_MODEL = 1024
BATCH = 4
SEQ = 8192
DEPTH = 4

PLE_DIM = 256
N_BRANCH = 4
BRANCH_WIDTH = 256
HEAD_DIM = 64
N_HEADS = BRANCH_WIDTH // HEAD_DIM
CONV_WIDTH = 3
ATTN_BLOCK = 128
GLA_CHUNK = 64
SPATIAL_CHUNK = 128
EPS = 1e-6
MASK_VALUE = -1e30
IN_COLS = 15 * BRANCH_WIDTH + N_HEADS + N_BRANCH * D_MODEL

kernel_name = "hybrid_conv_fox_hgrn2_gmlp_gated_merge"


def _split_points():
    W = BRANCH_WIDTH
    sizes = [W] * 4 + [W] * 4 + [N_HEADS] + [W] * 4 + [W] * 3 + [N_BRANCH * D_MODEL]
    return [int(s) for s in np.cumsum(sizes)[:-1]]


def rms_norm(x, g):
    xf = x.astype(jnp.float32)
    return xf * lax.rsqrt(jnp.mean(xf * xf, axis=-1, keepdims=True) + EPS) * g.astype(jnp.float32)


def group_rms_norm(x, g):
    Bn, S, W = x.shape
    xg = x.astype(jnp.float32).reshape(Bn, S, N_HEADS, HEAD_DIM)
    xg = xg * lax.rsqrt(jnp.mean(xg * xg, axis=-1, keepdims=True) + EPS)
    return (xg * g.astype(jnp.float32).reshape(N_HEADS, HEAD_DIM)).reshape(Bn, S, W)


def short_conv_mixer(x_in, b, c, w, bias):
    S = x_in.shape[1]
    z = c.astype(jnp.float32) * x_in.astype(jnp.float32)
    zp = jnp.pad(z, ((0, 0), (CONV_WIDTH - 1, 0), (0, 0)))
    wf = w.astype(jnp.float32)
    y = zp[:, 0:S] * wf[0]
    for tap in range(1, CONV_WIDTH):
        y = y + zp[:, tap:tap + S] * wf[tap]
    return b.astype(jnp.float32) * (y + bias.astype(jnp.float32))


def forgetting_attention(q, k, v, f_logit, gq, gk):
    Bn, S, _ = q.shape
    f32 = jnp.float32

    def heads(t):
        return t.astype(f32).reshape(Bn, S, N_HEADS, HEAD_DIM).transpose(0, 2, 1, 3)

    qh = rms_norm(heads(q), gq)
    kh = rms_norm(heads(k), gk)
    vh = heads(v)
    cum = jnp.cumsum(jax.nn.log_sigmoid(f_logit.astype(f32)).transpose(0, 2, 1), axis=-1)
    nb = S // ATTN_BLOCK
    qb = qh.reshape(Bn, N_HEADS, nb, ATTN_BLOCK, HEAD_DIM).transpose(2, 0, 1, 3, 4)
    cb = cum.reshape(Bn, N_HEADS, nb, ATTN_BLOCK).transpose(2, 0, 1, 3)
    kpos = jnp.arange(S)
    scale = HEAD_DIM ** -0.5

    def block(args):
        qi, ci, bi = args
        logits = jnp.einsum('bhqd,bhkd->bhqk', qi, kh) * scale + (ci[..., None] - cum[:, :, None, :])
        qpos = bi * ATTN_BLOCK + jnp.arange(ATTN_BLOCK)
        mask = qpos[:, None] >= kpos[None, :]
        probs = jax.nn.softmax(jnp.where(mask, logits, MASK_VALUE), axis=-1)
        return jnp.einsum('bhqk,bhkd->bhqd', probs, vh)

    o = lax.map(block, (qb, cb, jnp.arange(nb)))
    return o.transpose(1, 0, 3, 2, 4).reshape(Bn, S, N_HEADS * HEAD_DIM)


def hgrn2_recurrence(q, f_logit, i_in, lb, gain):
    Bn, S, W = q.shape
    f32 = jnp.float32
    qf = jax.nn.silu(q.astype(f32))
    fl = f_logit.astype(f32)
    lbf = lb.astype(f32)
    log_g = jnp.log(lbf + (1.0 - lbf) * jax.nn.sigmoid(fl))
    kf = (1.0 - lbf) * jax.nn.sigmoid(-fl)
    vf = i_in.astype(f32)
    nc = S // GLA_CHUNK

    def chunks(t):
        return t.reshape(Bn, nc, GLA_CHUNK, N_HEADS, HEAD_DIM).transpose(1, 0, 3, 2, 4)

    causal = jnp.tril(jnp.ones((GLA_CHUNK, GLA_CHUNK), dtype=bool))[:, :, None]

    def step(state, inp):
        qc, kc, vc, gc = inp
        b = jnp.cumsum(gc, axis=2)
        o_inter = jnp.einsum('bhtk,bhkv->bhtv', qc * jnp.exp(b), state)
        diff = b[:, :, :, None, :] - b[:, :, None, :, :]
        decay = jnp.where(causal, jnp.exp(jnp.where(causal, diff, 0.0)), 0.0)
        scores = jnp.einsum('bhtk,bhsk,bhtsk->bhts', qc, kc, decay)
        o_intra = jnp.einsum('bhts,bhsv->bhtv', scores, vc)
        b_last = b[:, :, -1]
        new_state = jnp.exp(b_last)[..., None] * state + jnp.einsum(
            'bhsk,bhsv->bhkv', kc * jnp.exp(b_last[:, :, None] - b), vc)
        return new_state, o_inter + o_intra

    state0 = jnp.zeros((Bn, N_HEADS, HEAD_DIM, HEAD_DIM), f32)
    _, o = lax.scan(step, state0, (chunks(qf), chunks(kf), chunks(vf), chunks(log_g)))
    o = o.transpose(1, 0, 3, 2, 4).reshape(Bn, S, W)
    return group_rms_norm(o, gain)


def spatial_gating_mixer(u, v, gv, w_s, b_s):
    Bn, S, W = u.shape
    vn = group_rms_norm(v, gv).reshape(Bn, S // SPATIAL_CHUNK, SPATIAL_CHUNK, N_HEADS, HEAD_DIM)
    causal = jnp.tril(jnp.ones((SPATIAL_CHUNK, SPATIAL_CHUNK), dtype=jnp.float32))
    w = w_s.astype(jnp.float32) * causal
    s = jnp.einsum('gts,bnsgc->bntgc', w, vn) + b_s.astype(jnp.float32).T[None, None, :, :, None]
    return u.astype(jnp.float32) * s.reshape(Bn, S, W)


def setup_inputs(seed: int = 0) -> dict:
    key = jax.random.key(seed)
    ks = jax.random.split(key, 24)
    W = BRANCH_WIDTH
    n = jax.random.normal
    f32 = jnp.float32
    return {
        "x": n(ks[0], (BATCH, SEQ, D_MODEL), f32),
        "p": n(ks[1], (DEPTH, BATCH, SEQ, PLE_DIM), f32),
        "norm_mix": 1.0 + 0.02 * n(ks[2], (DEPTH, D_MODEL), f32),
        "w_in": n(ks[3], (DEPTH, D_MODEL, IN_COLS), f32) * D_MODEL ** -0.5,
        "conv_w": n(ks[4], (DEPTH, CONV_WIDTH, W), f32) * CONV_WIDTH ** -0.5,
        "conv_b": 0.02 * n(ks[5], (DEPTH, W), f32),
        "fgate_bias": jnp.linspace(1.0, 4.0, N_HEADS, dtype=f32) + 0.1 * n(ks[6], (DEPTH, N_HEADS), f32),
        "q_norm": 1.0 + 0.02 * n(ks[7], (DEPTH, HEAD_DIM), f32),
        "k_norm": 1.0 + 0.02 * n(ks[8], (DEPTH, HEAD_DIM), f32),
        "lb_logits": 0.5 * n(ks[9], (DEPTH, W), f32),
        "hgrn_norm": 1.0 + 0.02 * n(ks[10], (DEPTH, W), f32),
        "sgu_norm": 1.0 + 0.02 * n(ks[11], (DEPTH, W), f32),
        "spatial_w": 0.5 * n(ks[12], (DEPTH, N_HEADS, SPATIAL_CHUNK, SPATIAL_CHUNK), f32) * SPATIAL_CHUNK ** -0.5,
        "spatial_b": 1.0 + 0.02 * n(ks[13], (DEPTH, N_HEADS, SPATIAL_CHUNK), f32),
        "w_up": n(ks[14], (DEPTH, N_BRANCH, W, D_MODEL), f32) * W ** -0.5,
        "merge_b": 0.02 * n(ks[15], (DEPTH, N_BRANCH, D_MODEL), f32),
        "w_o": n(ks[16], (DEPTH, D_MODEL, D_MODEL), f32) * (0.5 * D_MODEL ** -0.5),
        "norm_ple": 1.0 + 0.02 * n(ks[17], (DEPTH, D_MODEL), f32),
        "w_ple_gate": n(ks[18], (DEPTH, D_MODEL, D_MODEL), f32) * D_MODEL ** -0.5,
        "w_ple_proj": n(ks[19], (DEPTH, PLE_DIM, D_MODEL), f32) * (0.5 * PLE_DIM ** -0.5),
    }


def reference(x, p, norm_mix, w_in, conv_w, conv_b, fgate_bias, q_norm, k_norm, lb_logits,
              hgrn_norm, sgu_norm, spatial_w, spatial_b, w_up, merge_b, w_o, norm_ple,
              w_ple_gate, w_ple_proj):
    dt = x.dtype
    Bn, S, _ = x.shape
    splits = _split_points()
    lb_p = jax.nn.softmax(lb_logits.astype(jnp.float32), axis=0)
    lower_bounds = jnp.clip(jnp.cumsum(lb_p, axis=0) - lb_p[0], 0.0, 1.0)
    for li in range(DEPTH):
        h = rms_norm(x, norm_mix[li]).astype(dt)
        z = h @ w_in[li]
        (a_x, a_b, a_c, a_g,
         b_q, b_k, b_v, b_g, b_f,
         c_q, c_f, c_i, c_g,
         d_u, d_v, d_g, m_logits) = jnp.split(z, splits, axis=-1)

        y_a = short_conv_mixer(a_x, a_b, a_c, conv_w[li], conv_b[li]).astype(dt) * jax.nn.silu(a_g)
        y_b = forgetting_attention(b_q, b_k, b_v, b_f + fgate_bias[li], q_norm[li], k_norm[li]).astype(dt) * jax.nn.silu(b_g)
        y_c = hgrn2_recurrence(c_q, c_f, c_i, lower_bounds[li], hgrn_norm[li]).astype(dt) * jax.nn.silu(c_g)
        y_d = spatial_gating_mixer(d_u, d_v, sgu_norm[li], spatial_w[li], spatial_b[li]).astype(dt) * jax.nn.silu(d_g)

        branches = (y_a, y_b, y_c, y_d)
        gate_logits = m_logits.reshape(Bn, S, N_BRANCH, D_MODEL)
        merged = jax.nn.sigmoid(gate_logits[:, :, 0] + merge_b[li, 0]) * (branches[0] @ w_up[li, 0])
        for bi in range(1, N_BRANCH):
            merged = merged + jax.nn.sigmoid(gate_logits[:, :, bi] + merge_b[li, bi]) * (branches[bi] @ w_up[li, bi])
        x = x + merged @ w_o[li]

        hp = rms_norm(x, norm_ple[li]).astype(dt)
        x = x + jax.nn.sigmoid(hp @ w_ple_gate[li]) * (p[li] @ w_ple_proj[li])
    return x
```

```python
import functools
import math

import jax
import jax.numpy as jnp
import numpy as np
from jax import lax
from jax.experimental import pallas as pl
from jax.experimental.pallas import tpu as pltpu

F32 = jnp.float32
BF16 = jnp.bfloat16

D_MODEL = 1024
PLE_DIM = 256
N_BRANCH = 4
BRANCH_WIDTH = 256
HEAD_DIM = 64
N_HEADS = BRANCH_WIDTH // HEAD_DIM
CONV_WIDTH = 3
GLA_CHUNK = 64
SPATIAL_CHUNK = 128
EPS = 1e-6
MASK_VALUE = -1e30
LOG2E = math.log2(math.e)

Z_COLS = 8192
COL_MERGE = 0
COL_A = 16
COL_B = 20
COL_C = 24
COL_D = 28
COL_F = 31

AUG = 128
LANE_C = HEAD_DIM

VMEM_LIMIT_BYTES = 56 * 1024 * 1024
HGRN_SAFE_RANGE = 60.0


def _cparams(semantics):
    return pltpu.CompilerParams(dimension_semantics=semantics, vmem_limit_bytes=VMEM_LIMIT_BYTES)


def _split3(x):
    a = x.astype(BF16)
    r = x - a.astype(F32)
    b = r.astype(BF16)
    c = (r - b.astype(F32)).astype(BF16)
    return a, b, c


def _dot(a, b):
    return jnp.dot(a, b, preferred_element_type=F32)


def _dot_nt(a, b):
    return lax.dot_general(a, b, (((1,), (1,)), ((), ())), preferred_element_type=F32)


def _dot_tn(a, b):
    return lax.dot_general(a, b, (((0,), (0,)), ((), ())), preferred_element_type=F32)


def _iota(shape, dim):
    return lax.broadcasted_iota(jnp.int32, shape, dim)


def _head_block_mask(rows, cols):
    return (_iota((rows, cols), 0) // HEAD_DIM) == (_iota((rows, cols), 1) // HEAD_DIM)


def _group_mean_sq(x):
    w = x.shape[-1]
    avg = jnp.where(_head_block_mask(w, w), 1.0 / HEAD_DIM, 0.0).astype(BF16)
    sq = x * x
    hi = sq.astype(BF16)
    lo = (sq - hi.astype(F32)).astype(BF16)
    return _dot(hi, avg) + _dot(lo, avg)


def _in_proj_kernel(x_ref, g_ref, w_ref, z_ref, h_ref):
    @pl.when(pl.program_id(1) == 0)
    def _():
        x = x_ref[...]
        ms = jnp.mean(x * x, axis=-1, keepdims=True)
        h_ref[...] = (x * lax.rsqrt(ms + EPS) * g_ref[...]).astype(BF16)

    z_ref[...] = _dot(h_ref[...], w_ref[...]).astype(z_ref.dtype)


def _in_proj(x, g, w):
    t = x.shape[0]
    tm = min(1024, t)
    tn = 1024
    return pl.pallas_call(
        _in_proj_kernel,
        out_shape=jax.ShapeDtypeStruct((t, Z_COLS), BF16),
        grid=(t // tm, Z_COLS // tn),
        in_specs=[
            pl.BlockSpec((tm, D_MODEL), lambda i, j: (i, 0)),
            pl.BlockSpec((1, D_MODEL), lambda i, j: (0, 0)),
            pl.BlockSpec((D_MODEL, tn), lambda i, j: (0, j)),
        ],
        out_specs=pl.BlockSpec((tm, tn), lambda i, j: (i, j)),
        scratch_shapes=[pltpu.VMEM((tm, D_MODEL), BF16)],
        compiler_params=_cparams(("parallel", "arbitrary")),
        name="in_proj",
    )(x, g, w)


def _attn_prep_kernel(q_ref, k_ref, v_ref, f_ref, fb_ref, gq_ref, gk_ref,
                      qa_ref, ka_ref, va_ref, carry_ref):
    ts = q_ref.shape[1]

    @pl.when(pl.program_id(1) == 0)
    def _():
        carry_ref[...] = jnp.zeros_like(carry_ref)

    f = f_ref[0][:, :AUG].astype(F32) + fb_ref[...]
    log_f = (jnp.minimum(f, 0.0) - jnp.log1p(jnp.exp(-jnp.abs(f)))) * LOG2E
    tril = (_iota((ts, ts), 0) >= _iota((ts, ts), 1)).astype(BF16)
    l1, l2, l3 = _split3(log_f)
    cum = _dot(tril, l1) + _dot(tril, l2) + _dot(tril, l3) + carry_ref[...]
    carry_ref[...] = cum[ts - 1:ts, :]

    lane = _iota((ts, AUG), 1)
    in_head = lane < HEAD_DIM
    ones_q = (lane >= LANE_C + 3) & (lane < LANE_C + 6)
    ones_k = (lane >= LANE_C) & (lane < LANE_C + 3)

    def head_tile(ref, h):
        tile = ref[0][:, (h // 2) * AUG:(h // 2 + 1) * AUG].astype(F32)
        if h % 2:
            tile = pltpu.roll(tile, HEAD_DIM, axis=1)
        return jnp.where(in_head, tile, 0.0)

    def normed(tile, gain):
        ms = jnp.sum(tile * tile, axis=-1, keepdims=True) * (1.0 / HEAD_DIM)
        return tile * lax.rsqrt(ms + EPS) * gain

    for h in range(N_HEADS):
        c = cum[:, h:h + 1]
        c1 = c.astype(BF16).astype(F32)
        c2 = (c - c1).astype(BF16).astype(F32)
        c3 = c - c1 - c2
        qn = normed(head_tile(q_ref, h), gq_ref[...])
        qa = jnp.where(lane == LANE_C, c1, qn)
        qa = jnp.where(lane == LANE_C + 1, c2, qa)
        qa = jnp.where(lane == LANE_C + 2, c3, qa)
        qa = jnp.where(ones_q, 1.0, qa)
        qa_ref[0, h] = qa.astype(BF16)
        kn = normed(head_tile(k_ref, h), gk_ref[...])
        ka = jnp.where(lane == LANE_C + 3, -c1, kn)
        ka = jnp.where(lane == LANE_C + 4, -c2, ka)
        ka = jnp.where(lane == LANE_C + 5, -c3, ka)
        ka = jnp.where(ones_k, 1.0, ka)
        ka_ref[0, h] = ka.astype(BF16)
        va = jnp.where(lane == HEAD_DIM, 1.0, head_tile(v_ref, h))
        va_ref[0, h] = va.astype(BF16)


def _attn_prep(z3, fbias, gq, gk):
    b, s, _ = z3.shape
    ts = min(512, s)
    col = lambda c: pl.BlockSpec((1, ts, BRANCH_WIDTH), lambda bi, si, c=c: (bi, si, c))
    vec = pl.BlockSpec((1, AUG), lambda bi, si: (0, 0))
    aug = pl.BlockSpec((1, N_HEADS, ts, AUG), lambda bi, si: (bi, 0, si, 0))
    shape = jax.ShapeDtypeStruct((b, N_HEADS, s, AUG), BF16)
    return pl.pallas_call(
        _attn_prep_kernel,
        out_shape=(shape, shape, shape),
        grid=(b, s // ts),
        in_specs=[col(COL_B), col(COL_B + 1), col(COL_B + 2), col(COL_F), vec, vec, vec],
        out_specs=(aug, aug, aug),
        scratch_shapes=[pltpu.VMEM((1, AUG), F32)],
        compiler_params=_cparams(("parallel", "arbitrary")),
        name="attn_prep",
    )(z3, z3, z3, z3, fbias, gq, gk)


def _fox_attn_kernel(qi_ref, kj_ref, qa_ref, ka_ref, va_ref, o_ref, m_ref, acc_ref):
    p_idx = pl.program_id(1)
    qi = qi_ref[p_idx]
    kj = kj_ref[p_idx]
    tq = qa_ref.shape[2]
    tk = ka_ref.shape[2]

    @pl.when(kj == 0)
    def _():
        m_ref[...] = jnp.full_like(m_ref, MASK_VALUE)
        acc_ref[...] = jnp.zeros_like(acc_ref)

    def step(diagonal):
        for h in range(N_HEADS):
            s = _dot_nt(qa_ref[0, h], ka_ref[0, h])
            if diagonal:
                s = jnp.where(_iota((tq, tk), 1) <= _iota((tq, tk), 0), s, MASK_VALUE)
            m_prev = m_ref[h]
            m_new = jnp.maximum(m_prev, jnp.max(s, axis=-1, keepdims=True))
            alpha = jnp.exp2(m_prev - m_new)
            p = jnp.exp2(s - m_new[:, :1])
            acc_ref[h] = acc_ref[h] * alpha + _dot(p.astype(BF16), va_ref[0, h])
            m_ref[h] = m_new

    @pl.when(kj < qi)
    def _():
        step(False)

    @pl.when(kj == qi)
    def _():
        step(True)
        outs = []
        for h in range(N_HEADS):
            acc = acc_ref[h]
            outs.append(acc[:, :HEAD_DIM] / acc[:, HEAD_DIM:HEAD_DIM + 1])
        o_ref[0] = jnp.concatenate(outs, axis=-1)


def _fox_attn(qa, ka, va):
    b, _, s, _ = qa.shape
    tq = min(512, s)
    nq = s // tq
    pairs = [(i, j) for i in range(nq) for j in range(i + 1)]
    qi = jnp.asarray([p[0] for p in pairs], jnp.int32)
    kj = jnp.asarray([p[1] for p in pairs], jnp.int32)
    q_spec = pl.BlockSpec((1, N_HEADS, tq, AUG), lambda bi, p, qi, kj: (bi, 0, qi[p], 0))
    k_spec = pl.BlockSpec((1, N_HEADS, tq, AUG), lambda bi, p, qi, kj: (bi, 0, kj[p], 0))
    return pl.pallas_call(
        _fox_attn_kernel,
        out_shape=jax.ShapeDtypeStruct((b, s, BRANCH_WIDTH), F32),
        grid_spec=pltpu.PrefetchScalarGridSpec(
            num_scalar_prefetch=2,
            grid=(b, len(pairs)),
            in_specs=[q_spec, k_spec, k_spec],
            out_specs=pl.BlockSpec((1, tq, BRANCH_WIDTH), lambda bi, p, qi, kj: (bi, qi[p], 0)),
            scratch_shapes=[pltpu.VMEM((N_HEADS, tq, AUG), F32),
                            pltpu.VMEM((N_HEADS, tq, AUG), F32)],
        ),
        compiler_params=_cparams(("parallel", "arbitrary")),
        name="fox_attn",
    )(qi, kj, qa, ka, va)


def _hgrn_kernel(q_ref, f_ref, i_ref, lb_ref, gain_ref, o_ref,
                 state_ref, qs_ref, ks_ref, vs_ref, lg_ref, oc_ref):
    tc = q_ref.shape[1]
    c = GLA_CHUNK
    half = c // 2
    w = BRANCH_WIDTH

    @pl.when(pl.program_id(1) == 0)
    def _():
        state_ref[...] = jnp.zeros_like(state_ref)

    lb = lb_ref[...]
    bd_mask = _head_block_mask(w, w)
    row = _iota((c, w), 0)
    top = row < half
    tril_c = (_iota((c, c), 0) >= _iota((c, c), 1)).astype(BF16)
    causal = (_iota((c, w), 1) % HEAD_DIM) <= row

    def block_diag(x):
        return jnp.where(bd_mask, jnp.concatenate([x] * N_HEADS, axis=0), 0.0).astype(BF16)

    for ci in range(tc // c):
        rows = pl.ds(ci * c, c)
        q = jax.nn.silu(q_ref[0, rows, :].astype(F32))
        fl = f_ref[0, rows, :].astype(F32)
        v = i_ref[0, rows, :].astype(F32)
        log_g = jnp.log(lb + (1.0 - lb) * jax.nn.sigmoid(fl))
        kf = (1.0 - lb) * jax.nn.sigmoid(-fl)
        l1, l2, l3 = _split3(log_g)
        bsum = _dot(tril_c, l1) + _dot(tril_c, l2) + _dot(tril_c, l3)

        b_first, b_q1 = bsum[0:1], bsum[half // 2 - 1:half // 2]
        b_mid, b_mid1 = bsum[half - 1:half], bsum[half:half + 1]
        b_q3, b_last = bsum[half + half // 2 - 1:half + half // 2], bsum[c - 1:c]
        spread = jnp.maximum(jnp.maximum(b_first - b_q1, b_q1 - b_mid),
                             jnp.maximum(b_mid1 - b_q3, b_q3 - b_last))
        safe = jnp.max(spread) <= HGRN_SAFE_RANGE
        state_t = state_ref[...]
        e_last = jnp.exp(b_last)

        @pl.when(safe)
        def _():
            ref_d = jnp.where(top, b_q1, b_q3)
            qd = q * jnp.exp(bsum - ref_d)
            kd = kf * jnp.exp(ref_d - bsum)
            qo = jnp.where(top, 0.0, q * jnp.exp(jnp.minimum(bsum - b_mid, 0.0)))
            ko = jnp.where(top, kf * jnp.exp(jnp.minimum(b_mid - bsum, 0.0)), 0.0)
            qcat = jnp.concatenate(
                [qo, jnp.where(top, qd, 0.0), jnp.where(top, 0.0, qd)], axis=1).astype(BF16)
            kcat = jnp.concatenate(
                [block_diag(ko), block_diag(jnp.where(top, kd, 0.0)),
                 block_diag(jnp.where(top, 0.0, kd))], axis=1)
            scores = jnp.where(causal, _dot_nt(qcat, kcat), 0.0)
            o_intra = _dot(scores.astype(BF16), block_diag(v))
            o_inter = _dot_nt((q * jnp.exp(bsum)).astype(BF16), state_t.astype(BF16))
            oc_ref[rows, :] = o_intra + o_inter
            k_end = (kf * jnp.exp(b_last - bsum)).astype(BF16)
            upd = _dot_tn(v.astype(BF16), k_end)
            state_ref[...] = state_t * e_last + jnp.where(bd_mask, upd, 0.0)

        @pl.when(jnp.logical_not(safe))
        def _():
            qs_ref[...] = q
            ks_ref[...] = kf
            vs_ref[...] = v
            lg_ref[...] = log_g

            def body(t, carry):
                g_t = jnp.exp(lg_ref[pl.ds(t, 1), :])
                outer = _dot_tn(vs_ref[pl.ds(t, 1), :].astype(BF16),
                                ks_ref[pl.ds(t, 1), :].astype(BF16))
                st = state_ref[...] * g_t + jnp.where(bd_mask, outer, 0.0)
                state_ref[...] = st
                oc_ref[pl.ds(ci * c + t, 1), :] = _dot_nt(
                    qs_ref[pl.ds(t, 1), :].astype(BF16), st.astype(BF16))
                return carry

            lax.fori_loop(0, c, body, 0)

    o = oc_ref[...]
    o_ref[0] = o * lax.rsqrt(_group_mean_sq(o) + EPS) * gain_ref[...]


def _hgrn(z3, lb, gain):
    b, s, _ = z3.shape
    tc = min(256, s)
    col = lambda c: pl.BlockSpec((1, tc, BRANCH_WIDTH), lambda bi, si, c=c: (bi, si, c))
    vec = pl.BlockSpec((1, BRANCH_WIDTH), lambda bi, si: (0, 0))
    chunk = pltpu.VMEM((GLA_CHUNK, BRANCH_WIDTH), F32)
    return pl.pallas_call(
        _hgrn_kernel,
        out_shape=jax.ShapeDtypeStruct((b, s, BRANCH_WIDTH), F32),
        grid=(b, s // tc),
        in_specs=[col(COL_C), col(COL_C + 1), col(COL_C + 2), vec, vec],
        out_specs=pl.BlockSpec((1, tc, BRANCH_WIDTH), lambda bi, si: (bi, si, 0)),
        scratch_shapes=[pltpu.VMEM((BRANCH_WIDTH, BRANCH_WIDTH), F32),
                        chunk, chunk, chunk, chunk,
                        pltpu.VMEM((tc, BRANCH_WIDTH), F32)],
        compiler_params=_cparams(("parallel", "arbitrary")),
        name="hgrn2",
    )(z3, z3, z3, lb, gain)


HALO = 16


def _merge_kernel(zm_ref, za_ref, du_ref, dv_ref, dg_ref, bg_ref, cg_ref, hx_ref, hc_ref, ob_ref, oc_ref,
                  x_ref, p_ref, cw_ref, cb_ref, gv_ref, ws_ref, bs_ref, wup_ref, mb_ref,
                  wo_ref, gp_ref, wg_ref, wp_ref, out_ref, *, tiles_per_seq):
    tm = x_ref.shape[0]
    w = BRANCH_WIDTH

    za = za_ref[...].astype(F32)
    zc = za[:, 2 * w:3 * w] * za[:, 0:w]
    halo = hc_ref[...].astype(F32) * hx_ref[...].astype(F32)
    halo = jnp.where(pl.program_id(0) % tiles_per_seq == 0, 0.0, halo)
    ext = jnp.concatenate([halo, zc], axis=0)
    conv = zc * cw_ref[2:3, :]
    for tap in range(CONV_WIDTH - 1):
        shift = CONV_WIDTH - 1 - tap
        conv = conv + pltpu.roll(ext, shift, axis=0)[HALO:] * cw_ref[tap:tap + 1, :]
    y_a = za[:, w:2 * w] * (conv + cb_ref[...]) * jax.nn.silu(za[:, 3 * w:4 * w])

    y_b = ob_ref[...] * jax.nn.silu(bg_ref[...].astype(F32))
    y_c = oc_ref[...] * jax.nn.silu(cg_ref[...].astype(F32))

    vd = dv_ref[...].astype(F32)
    vn = (vd * lax.rsqrt(_group_mean_sq(vd) + EPS) * gv_ref[...])
    lane_head = _iota((SPATIAL_CHUNK, w), 1) // HEAD_DIM
    mixed = []
    for ci in range(tm // SPATIAL_CHUNK):
        vc = vn[ci * SPATIAL_CHUNK:(ci + 1) * SPATIAL_CHUNK]
        stacked = jnp.concatenate(
            [jnp.where(lane_head == g, vc, 0.0) for g in range(N_HEADS)], axis=0).astype(BF16)
        mixed.append(_dot(ws_ref[...], stacked) + bs_ref[...])
    y_d = (du_ref[...].astype(F32) * jnp.concatenate(mixed, axis=0)
           * jax.nn.silu(dg_ref[...].astype(F32)))

    merged = None
    for bi, y in enumerate((y_a, y_b, y_c, y_d)):
        gate = jax.nn.sigmoid(
            zm_ref[:, bi * D_MODEL:(bi + 1) * D_MODEL].astype(F32) + mb_ref[bi:bi + 1, :])
        term = gate * _dot(y.astype(BF16), wup_ref[bi])
        merged = term if merged is None else merged + term
    x1 = x_ref[...] + _dot(merged.astype(BF16), wo_ref[...])

    ms = jnp.mean(x1 * x1, axis=-1, keepdims=True)
    hp = (x1 * lax.rsqrt(ms + EPS) * gp_ref[...]).astype(BF16)
    ple = _dot(p_ref[...].astype(BF16), wp_ref[...])
    out_ref[...] = x1 + jax.nn.sigmoid(_dot(hp, wg_ref[...])) * ple


def _merge(z, o_b, o_c, x, p, seq_len, cw, cb, gv, ws, bs, wup, mb, wo, gp, wg, wp):
    t = x.shape[0]
    tm = min(256, seq_len)
    w = BRANCH_WIDTH
    hstep = tm // HALO
    row = lambda width, c: pl.BlockSpec((tm, width), lambda i, c=c: (i, c))
    halo = lambda c: pl.BlockSpec((HALO, w), lambda i, c=c: (jnp.maximum(i * hstep - 1, 0), c))
    full = lambda a: pl.BlockSpec(a.shape, lambda i, n=a.ndim: (0,) * n)
    consts = (cw, cb, gv, ws, bs, wup, mb, wo, gp, wg, wp)
    return pl.pallas_call(
        functools.partial(_merge_kernel, tiles_per_seq=seq_len // tm),
        out_shape=jax.ShapeDtypeStruct((t, D_MODEL), F32),
        grid=(t // tm,),
        in_specs=[
            row(N_BRANCH * D_MODEL, COL_MERGE),
            row(4 * w, COL_A // 4),
            row(w, COL_D), row(w, COL_D + 1), row(w, COL_D + 2),
            row(w, COL_B + 3), row(w, COL_C + 3),
            halo(COL_A), halo(COL_A + 2),
            row(w, 0), row(w, 0),
            row(D_MODEL, 0), row(PLE_DIM, 0),
        ] + [full(a) for a in consts],
        out_specs=pl.BlockSpec((tm, D_MODEL), lambda i: (i, 0)),
        compiler_params=_cparams(("parallel",)),
        name="merge",
    )(z, z, z, z, z, z, z, z, z, o_b, o_c, x, p, *consts)


def _reorder_w_in(w_in):
    w = BRANCH_WIDTH
    n_pre = 8 * w
    n_f = N_HEADS
    n_cd = 7 * w
    pre = w_in[:, :, :n_pre]
    fcols = w_in[:, :, n_pre:n_pre + n_f]
    cd = w_in[:, :, n_pre + n_f:n_pre + n_f + n_cd]
    merge = w_in[:, :, n_pre + n_f + n_cd:]
    pad = jnp.zeros(w_in.shape[:2] + (w - n_f,), w_in.dtype)
    return jnp.concatenate([merge, pre, cd, fcols, pad], axis=-1).astype(BF16)


def kernel(x, p, norm_mix, w_in, conv_w, conv_b, fgate_bias, q_norm, k_norm, lb_logits,
           hgrn_norm, sgu_norm, spatial_w, spatial_b, w_up, merge_b, w_o, norm_ple,
           w_ple_gate, w_ple_proj):
    bn, s, _ = x.shape
    depth = w_in.shape[0]
    t = bn * s
    assert w_in.shape[-1] == 15 * BRANCH_WIDTH + N_HEADS + N_BRANCH * D_MODEL
    assert s % SPATIAL_CHUNK == 0 and s % GLA_CHUNK == 0

    lb_p = jax.nn.softmax(lb_logits.astype(F32), axis=0)
    lower_bounds = jnp.clip(jnp.cumsum(lb_p, axis=0) - lb_p[0], 0.0, 1.0)
    w_z = _reorder_w_in(w_in)
    fbias = jnp.pad(fgate_bias.astype(F32), ((0, 0), (0, AUG - N_HEADS)))
    gq = jnp.pad(q_norm.astype(F32) * (HEAD_DIM ** -0.5 * LOG2E), ((0, 0), (0, AUG - HEAD_DIM)))
    gk = jnp.pad(k_norm.astype(F32), ((0, 0), (0, AUG - HEAD_DIM)))
    causal = jnp.tril(jnp.ones((SPATIAL_CHUNK, SPATIAL_CHUNK), F32))
    ws = (spatial_w.astype(F32) * causal).transpose(0, 2, 1, 3).reshape(
        depth, SPATIAL_CHUNK, N_HEADS * SPATIAL_CHUNK).astype(BF16)
    bs = jnp.repeat(spatial_b.astype(F32).transpose(0, 2, 1), HEAD_DIM, axis=-1)
    w_up_b = w_up.astype(BF16)
    w_o_b = w_o.astype(BF16)
    w_g_b = w_ple_gate.astype(BF16)
    w_p_b = w_ple_proj.astype(BF16)

    xf = x.reshape(t, D_MODEL)
    for li in range(depth):
        z = _in_proj(xf, norm_mix[li][None], w_z[li])
        z3 = z.reshape(bn, s, Z_COLS)
        qa, ka, va = _attn_prep(z3, fbias[li][None], gq[li][None], gk[li][None])
        o_b = _fox_attn(qa, ka, va).reshape(t, BRANCH_WIDTH)
        o_c = _hgrn(z3, lower_bounds[li][None], hgrn_norm[li][None]).reshape(t, BRANCH_WIDTH)
        xf = _merge(z, o_b, o_c, xf, p[li].reshape(t, PLE_DIM), s,
                    conv_w[li], conv_b[li][None], sgu_norm[li][None], ws[li], bs[li],
                    w_up_b[li], merge_b[li], w_o_b[li], norm_ple[li][None], w_g_b[li], w_p_b[li])
    return xf.reshape(bn, s, D_MODEL).astype(x.dtype)
```

```python
import functools
import math

import jax
import jax.numpy as jnp
import numpy as np
from jax import lax
from jax.experimental import pallas as pl
from jax.experimental.pallas import tpu as pltpu

F32 = jnp.float32
BF16 = jnp.bfloat16

D_MODEL = 1024
PLE_DIM = 256
N_BRANCH = 4
BRANCH_WIDTH = 256
HEAD_DIM = 64
N_HEADS = BRANCH_WIDTH // HEAD_DIM
CONV_WIDTH = 3
GLA_CHUNK = 64
SPATIAL_CHUNK = 128
EPS = 1e-6
MASK_VALUE = -1e30
LOG2E = math.log2(math.e)

Z_COLS = 8192
COL_MERGE = 0
COL_A = 16
COL_B = 20
COL_C = 24
COL_D = 28
COL_F = 31

AUG = 128
LANE_C = HEAD_DIM
ATTN_BLOCK = 512
ATTN_STREAMS = 4
ATTN_WIDE = 2
ATTN_CUTOFF_LOG2 = 152.0
BF16_NORM_MARGIN = 1.02

VMEM_LIMIT_BYTES = 56 * 1024 * 1024
HGRN_SAFE_RANGE = 60.0


def _attn_block(seq_len):
    return min(ATTN_BLOCK, seq_len)


def _cparams(semantics):
    return pltpu.CompilerParams(dimension_semantics=semantics, vmem_limit_bytes=VMEM_LIMIT_BYTES)


def _split3(x):
    a = x.astype(BF16)
    r = x - a.astype(F32)
    b = r.astype(BF16)
    c = (r - b.astype(F32)).astype(BF16)
    return a, b, c


def _dot(a, b):
    return jnp.dot(a, b, preferred_element_type=F32)


def _dot_nt(a, b):
    return lax.dot_general(a, b, (((1,), (1,)), ((), ())), preferred_element_type=F32)


def _dot_tn(a, b):
    return lax.dot_general(a, b, (((0,), (0,)), ((), ())), preferred_element_type=F32)


def _iota(shape, dim):
    return lax.broadcasted_iota(jnp.int32, shape, dim)


def _head_block_mask(rows, cols):
    return (_iota((rows, cols), 0) // HEAD_DIM) == (_iota((rows, cols), 1) // HEAD_DIM)


def _group_mean_sq(x):
    w = x.shape[-1]
    avg = jnp.where(_head_block_mask(w, w), 1.0 / HEAD_DIM, 0.0).astype(BF16)
    sq = x * x
    hi = sq.astype(BF16)
    lo = (sq - hi.astype(F32)).astype(BF16)
    return _dot(hi, avg) + _dot(lo, avg)


def _in_proj_kernel(x_ref, g_ref, w_ref, z_ref, h_ref):
    @pl.when(pl.program_id(1) == 0)
    def _():
        x = x_ref[...]
        ms = jnp.mean(x * x, axis=-1, keepdims=True)
        h_ref[...] = (x * lax.rsqrt(ms + EPS) * g_ref[...]).astype(BF16)

    z_ref[...] = _dot(h_ref[...], w_ref[...]).astype(z_ref.dtype)


def _in_proj(x, g, w):
    t = x.shape[0]
    tm = min(1024, t)
    tn = 1024
    return pl.pallas_call(
        _in_proj_kernel,
        out_shape=jax.ShapeDtypeStruct((t, Z_COLS), BF16),
        grid=(t // tm, Z_COLS // tn),
        in_specs=[
            pl.BlockSpec((tm, D_MODEL), lambda i, j: (i, 0)),
            pl.BlockSpec((1, D_MODEL), lambda i, j: (0, 0)),
            pl.BlockSpec((D_MODEL, tn), lambda i, j: (0, j)),
        ],
        out_specs=pl.BlockSpec((tm, tn), lambda i, j: (i, j)),
        scratch_shapes=[pltpu.VMEM((tm, D_MODEL), BF16)],
        compiler_params=_cparams(("parallel", "arbitrary")),
        name="in_proj",
    )(x, g, w)


def _attn_prep_kernel(q_ref, k_ref, v_ref, f_ref, fb_ref, gq_ref, gk_ref,
                      qa_ref, ka_ref, va_ref, edge_ref, carry_ref):
    ts = q_ref.shape[1]

    @pl.when(pl.program_id(1) == 0)
    def _():
        carry_ref[...] = jnp.zeros_like(carry_ref)

    f = f_ref[0][:, :AUG].astype(F32) + fb_ref[...]
    log_f = (jnp.minimum(f, 0.0) - jnp.log1p(jnp.exp(-jnp.abs(f)))) * LOG2E
    tril = (_iota((ts, ts), 0) >= _iota((ts, ts), 1)).astype(BF16)
    l1, l2, l3 = _split3(log_f)
    cum = _dot(tril, l1) + _dot(tril, l2) + _dot(tril, l3) + carry_ref[...]
    carry_ref[...] = cum[ts - 1:ts, :]
    edge_row = _iota((8, AUG), 0)
    edge_ref[0, 0] = jnp.where(edge_row == 0, cum[0:1, :],
                               jnp.where(edge_row == 1, cum[ts - 1:ts, :], 0.0))

    lane = _iota((ts, AUG), 1)
    in_head = lane < HEAD_DIM
    ones_q = (lane >= LANE_C + 3) & (lane < LANE_C + 6)
    ones_k = (lane >= LANE_C) & (lane < LANE_C + 3)

    def head_tile(ref, h):
        tile = ref[0][:, (h // 2) * AUG:(h // 2 + 1) * AUG].astype(F32)
        if h % 2:
            tile = pltpu.roll(tile, HEAD_DIM, axis=1)
        return jnp.where(in_head, tile, 0.0)

    def normed(tile, gain):
        ms = jnp.sum(tile * tile, axis=-1, keepdims=True) * (1.0 / HEAD_DIM)
        return tile * lax.rsqrt(ms + EPS) * gain

    for h in range(N_HEADS):
        c = cum[:, h:h + 1]
        c1 = c.astype(BF16).astype(F32)
        c2 = (c - c1).astype(BF16).astype(F32)
        c3 = c - c1 - c2
        qn = normed(head_tile(q_ref, h), gq_ref[...])
        qa = jnp.where(lane == LANE_C, c1, qn)
        qa = jnp.where(lane == LANE_C + 1, c2, qa)
        qa = jnp.where(lane == LANE_C + 2, c3, qa)
        qa = jnp.where(ones_q, 1.0, qa)
        qa_ref[0, h] = qa.astype(BF16)
        kn = normed(head_tile(k_ref, h), gk_ref[...])
        ka = jnp.where(lane == LANE_C + 3, -c1, kn)
        ka = jnp.where(lane == LANE_C + 4, -c2, ka)
        ka = jnp.where(lane == LANE_C + 5, -c3, ka)
        ka = jnp.where(ones_k, 1.0, ka)
        ka_ref[0, h] = ka.astype(BF16)
        va = jnp.where(lane == HEAD_DIM, 1.0, head_tile(v_ref, h))
        va_ref[0, h] = va.astype(BF16)


def _attn_prep(z3, fbias, gq, gk):
    b, s, _ = z3.shape
    ts = _attn_block(s)
    col = lambda c: pl.BlockSpec((1, ts, BRANCH_WIDTH), lambda bi, si, c=c: (bi, si, c))
    vec = pl.BlockSpec((1, AUG), lambda bi, si: (0, 0))
    aug = pl.BlockSpec((1, N_HEADS, ts, AUG), lambda bi, si: (bi, 0, si, 0))
    shape = jax.ShapeDtypeStruct((b, N_HEADS, s, AUG), BF16)
    edge_shape = jax.ShapeDtypeStruct((b, s // ts, 8, AUG), F32)
    return pl.pallas_call(
        _attn_prep_kernel,
        out_shape=(shape, shape, shape, edge_shape),
        grid=(b, s // ts),
        in_specs=[col(COL_B), col(COL_B + 1), col(COL_B + 2), col(COL_F), vec, vec, vec],
        out_specs=(aug, aug, aug, pl.BlockSpec((1, 1, 8, AUG), lambda bi, si: (bi, si, 0, 0))),
        scratch_shapes=[pltpu.VMEM((1, AUG), F32)],
        compiler_params=_cparams(("parallel", "arbitrary")),
        name="attn_prep",
    )(z3, z3, z3, z3, fbias, gq, gk)


def _fox_attn_kernel(cfirst_ref, clast_ref, qa_ref, ka_ref, va_ref, o_ref, m_ref, acc_ref):
    bi = pl.program_id(0)
    qi = pl.program_id(1)
    nblk = pl.num_programs(1)
    tq = qa_ref.shape[2]
    tr = tq // ATTN_STREAMS

    def process(h, kj, nblocks, diagonal):
        start = pl.multiple_of(kj * tq, tq)
        widths = [(nblocks - 1) * tq + (r + 1) * tr if diagonal else nblocks * tq
                  for r in range(ATTN_STREAMS)]
        scores = [
            _dot_nt(qa_ref[0, h, r * tr:(r + 1) * tr, :],
                    ka_ref[0, h, pl.ds(start, widths[r]), :])
            for r in range(ATTN_STREAMS)]
        for r, s in enumerate(scores):
            wk = widths[r]
            if diagonal:
                edge = s[:, wk - tr:]
                edge = jnp.where(_iota((tr, tr), 1) <= _iota((tr, tr), 0), edge, MASK_VALUE)
                s = edge if wk == tr else jnp.concatenate([s[:, :wk - tr], edge], axis=1)
            m_prev = m_ref[r]
            m_new = jnp.maximum(m_prev, jnp.max(s, axis=-1, keepdims=True))
            alpha = jnp.exp2(m_prev - m_new)
            p = jnp.concatenate(
                [jnp.exp2(s[:, c * AUG:(c + 1) * AUG] - m_new) for c in range(wk // AUG)], axis=1)
            acc_ref[r] = acc_ref[r] * alpha + _dot(
                p.astype(BF16), va_ref[0, h, pl.ds(start, wk), :])
            m_ref[r] = m_new

    outs = []
    for h in range(N_HEADS):
        base = (bi * N_HEADS + h) * nblk
        c_i = cfirst_ref[base + qi]
        j_lo = lax.while_loop(
            lambda j: (j > 0) & (c_i - clast_ref[base + jnp.maximum(j - 1, 0)] >= 0.0),
            lambda j: j - 1, qi)
        m_ref[...] = jnp.full_like(m_ref, MASK_VALUE)
        acc_ref[...] = jnp.zeros_like(acc_ref)

        has_prev = j_lo < qi
        last_off = qi - has_prev.astype(jnp.int32)
        n_wide = (last_off - j_lo) // ATTN_WIDE

        def wide_body(t, carry, h=h, j_lo=j_lo):
            process(h, j_lo + t * ATTN_WIDE, ATTN_WIDE, False)
            return carry

        def single_body(j, carry, h=h):
            process(h, j, 1, False)
            return carry

        lax.fori_loop(0, n_wide, wide_body, 0)
        lax.fori_loop(j_lo + n_wide * ATTN_WIDE, last_off, single_body, 0)

        @pl.when(has_prev)
        def _(h=h):
            process(h, qi - 1, 2, True)

        @pl.when(jnp.logical_not(has_prev))
        def _(h=h):
            process(h, qi, 1, True)

        acc = acc_ref[...].reshape(tq, AUG)
        outs.append(acc[:, :HEAD_DIM] / acc[:, HEAD_DIM:HEAD_DIM + 1])
    o_ref[0] = jnp.concatenate(outs, axis=-1)


def _fox_attn(qa, ka, va, cfirst, clast):
    b, _, s, _ = qa.shape
    tq = _attn_block(s)
    q_spec = pl.BlockSpec((1, N_HEADS, tq, AUG), lambda bi, i, cf, cl: (bi, 0, i, 0))
    kv_spec = pl.BlockSpec((1, N_HEADS, s, AUG), lambda bi, i, cf, cl: (bi, 0, 0, 0))
    return pl.pallas_call(
        _fox_attn_kernel,
        out_shape=jax.ShapeDtypeStruct((b, s, BRANCH_WIDTH), F32),
        grid_spec=pltpu.PrefetchScalarGridSpec(
            num_scalar_prefetch=2,
            grid=(b, s // tq),
            in_specs=[q_spec, kv_spec, kv_spec],
            out_specs=pl.BlockSpec((1, tq, BRANCH_WIDTH), lambda bi, i, cf, cl: (bi, i, 0)),
            scratch_shapes=[pltpu.VMEM((ATTN_STREAMS, tq // ATTN_STREAMS, AUG), F32),
                            pltpu.VMEM((ATTN_STREAMS, tq // ATTN_STREAMS, AUG), F32)],
        ),
        compiler_params=_cparams(("parallel", "arbitrary")),
        name="fox_attn",
    )(cfirst, clast, qa, ka, va)


def _hgrn_kernel(q_ref, f_ref, i_ref, lb_ref, gain_ref, o_ref,
                 state_ref, qs_ref, ks_ref, vs_ref, lg_ref, oc_ref):
    tc = q_ref.shape[1]
    c = GLA_CHUNK
    half = c // 2
    w = BRANCH_WIDTH

    @pl.when(pl.program_id(1) == 0)
    def _():
        state_ref[...] = jnp.zeros_like(state_ref)

    n = tc // c
    lb = lb_ref[...]
    bd_mask = _head_block_mask(w, w)

    q = jax.nn.silu(q_ref[0].astype(F32))
    fl = f_ref[0].astype(F32)
    v = i_ref[0].astype(F32)
    t_small = jnp.exp(-jnp.abs(fl))
    s_big = 1.0 / (1.0 + t_small)
    s_small = t_small * s_big
    pos = fl >= 0.0
    log_g = jnp.log(lb + (1.0 - lb) * jnp.where(pos, s_big, s_small))
    kf = (1.0 - lb) * jnp.where(pos, s_small, s_big)

    r_i, c_i = _iota((tc, tc), 0), _iota((tc, tc), 1)
    tril = ((r_i >= c_i) & (r_i // c == c_i // c)).astype(BF16)
    l1, l2, l3 = _split3(log_g)
    bsum = _dot(tril, l1) + _dot(tril, l2) + _dot(tril, l3)

    b3 = bsum.reshape(n, c, w)
    b_first, b_q1 = b3[:, 0:1], b3[:, half // 2 - 1:half // 2]
    b_mid, b_mid1 = b3[:, half - 1:half], b3[:, half:half + 1]
    b_q3, b_last = b3[:, half + half // 2 - 1:half + half // 2], b3[:, c - 1:c]
    spread = jnp.maximum(jnp.maximum(b_first - b_q1, b_q1 - b_mid),
                         jnp.maximum(b_mid1 - b_q3, b_q3 - b_last))
    safe = jnp.max(spread) <= HGRN_SAFE_RANGE

    @pl.when(safe)
    def _():
        top = _iota((n, c, w), 1) < half
        ref_d = jnp.where(top, b_q1, b_q3)
        q3, k3 = q.reshape(n, c, w), kf.reshape(n, c, w)
        qd = q3 * jnp.exp(b3 - ref_d)
        kd = k3 * jnp.exp(ref_d - b3)
        qo = jnp.where(top, 0.0, q3 * jnp.exp(jnp.minimum(b3 - b_mid, 0.0)))
        ko = jnp.where(top, k3 * jnp.exp(jnp.minimum(b_mid - b3, 0.0)), 0.0)
        qcat = jnp.concatenate(
            [qo, jnp.where(top, qd, 0.0), jnp.where(top, 0.0, qd)], axis=2).astype(BF16)
        kparts = [x.astype(BF16) for x in (ko, jnp.where(top, kd, 0.0), jnp.where(top, 0.0, kd))]
        q_in = (q3 * jnp.exp(b3)).astype(BF16)
        k_end = (k3 * jnp.exp(b_last - b3)).astype(BF16)
        e_last = jnp.exp(b_last)
        v_b = v.astype(BF16).reshape(n, c, w)
        causal = (_iota((c, w), 1) % HEAD_DIM) <= _iota((c, w), 0)

        def block_diag(x):
            return jnp.where(bd_mask, jnp.concatenate([x] * N_HEADS, axis=0), 0.0)

        scores = []
        for ci in range(n):
            kcat = jnp.concatenate([block_diag(kp[ci]) for kp in kparts], axis=1)
            scores.append(jnp.where(causal, _dot_nt(qcat[ci], kcat), 0.0).astype(BF16))
        updates = [jnp.where(bd_mask, _dot_tn(v_b[ci], k_end[ci]), 0.0) for ci in range(n)]
        o_intra = [_dot(scores[ci], block_diag(v_b[ci])) for ci in range(n)]
        state_t = state_ref[...]
        outs = []
        for ci in range(n):
            outs.append(o_intra[ci] + _dot_nt(q_in[ci], state_t.astype(BF16)))
            state_t = state_t * e_last[ci] + updates[ci]
        state_ref[...] = state_t
        oc_ref[...] = jnp.concatenate(outs, axis=0)

    @pl.when(jnp.logical_not(safe))
    def _():
        qs_ref[...] = q
        ks_ref[...] = kf
        vs_ref[...] = v
        lg_ref[...] = log_g

        def body(t, carry):
            g_t = jnp.exp(lg_ref[pl.ds(t, 1), :])
            outer = _dot_tn(vs_ref[pl.ds(t, 1), :].astype(BF16),
                            ks_ref[pl.ds(t, 1), :].astype(BF16))
            st = state_ref[...] * g_t + jnp.where(bd_mask, outer, 0.0)
            state_ref[...] = st
            oc_ref[pl.ds(t, 1), :] = _dot_nt(qs_ref[pl.ds(t, 1), :].astype(BF16), st.astype(BF16))
            return carry

        lax.fori_loop(0, tc, body, 0)

    o = oc_ref[...]
    o_ref[0] = o * lax.rsqrt(_group_mean_sq(o) + EPS) * gain_ref[...]


def _hgrn(z3, lb, gain):
    b, s, _ = z3.shape
    tc = min(512, s)
    col = lambda c: pl.BlockSpec((1, tc, BRANCH_WIDTH), lambda bi, si, c=c: (bi, si, c))
    vec = pl.BlockSpec((1, BRANCH_WIDTH), lambda bi, si: (0, 0))
    tile = pltpu.VMEM((tc, BRANCH_WIDTH), F32)
    return pl.pallas_call(
        _hgrn_kernel,
        out_shape=jax.ShapeDtypeStruct((b, s, BRANCH_WIDTH), F32),
        grid=(b, s // tc),
        in_specs=[col(COL_C), col(COL_C + 1), col(COL_C + 2), vec, vec],
        out_specs=pl.BlockSpec((1, tc, BRANCH_WIDTH), lambda bi, si: (bi, si, 0)),
        scratch_shapes=[pltpu.VMEM((BRANCH_WIDTH, BRANCH_WIDTH), F32),
                        tile, tile, tile, tile, tile],
        compiler_params=_cparams(("parallel", "arbitrary")),
        name="hgrn2",
    )(z3, z3, z3, lb, gain)


HALO = 16


def _merge_kernel(zm_ref, za_ref, du_ref, dv_ref, dg_ref, bg_ref, cg_ref, hx_ref, hc_ref, ob_ref, oc_ref,
                  x_ref, p_ref, cw_ref, cb_ref, gv_ref, ws_ref, bs_ref, wup_ref, mb_ref,
                  wo_ref, gp_ref, wg_ref, wp_ref, out_ref, *, tiles_per_seq):
    tm = x_ref.shape[0]
    w = BRANCH_WIDTH

    za = za_ref[...].astype(F32)
    zc = za[:, 2 * w:3 * w] * za[:, 0:w]
    halo = hc_ref[...].astype(F32) * hx_ref[...].astype(F32)
    halo = jnp.where(pl.program_id(0) % tiles_per_seq == 0, 0.0, halo)
    ext = jnp.concatenate([halo, zc], axis=0)
    conv = zc * cw_ref[2:3, :]
    for tap in range(CONV_WIDTH - 1):
        shift = CONV_WIDTH - 1 - tap
        conv = conv + pltpu.roll(ext, shift, axis=0)[HALO:] * cw_ref[tap:tap + 1, :]
    y_a = za[:, w:2 * w] * (conv + cb_ref[...]) * jax.nn.silu(za[:, 3 * w:4 * w])

    y_b = ob_ref[...] * jax.nn.silu(bg_ref[...].astype(F32))
    y_c = oc_ref[...] * jax.nn.silu(cg_ref[...].astype(F32))

    vd = dv_ref[...].astype(F32)
    vn = (vd * lax.rsqrt(_group_mean_sq(vd) + EPS) * gv_ref[...])
    lane_head = _iota((SPATIAL_CHUNK, w), 1) // HEAD_DIM
    mixed = []
    for ci in range(tm // SPATIAL_CHUNK):
        vc = vn[ci * SPATIAL_CHUNK:(ci + 1) * SPATIAL_CHUNK]
        stacked = jnp.concatenate(
            [jnp.where(lane_head == g, vc, 0.0) for g in range(N_HEADS)], axis=0).astype(BF16)
        mixed.append(_dot(ws_ref[...], stacked) + bs_ref[...])
    y_d = (du_ref[...].astype(F32) * jnp.concatenate(mixed, axis=0)
           * jax.nn.silu(dg_ref[...].astype(F32)))

    merged = None
    for bi, y in enumerate((y_a, y_b, y_c, y_d)):
        gate = jax.nn.sigmoid(
            zm_ref[:, bi * D_MODEL:(bi + 1) * D_MODEL].astype(F32) + mb_ref[bi:bi + 1, :])
        term = gate * _dot(y.astype(BF16), wup_ref[bi])
        merged = term if merged is None else merged + term
    x1 = x_ref[...] + _dot(merged.astype(BF16), wo_ref[...])

    ms = jnp.mean(x1 * x1, axis=-1, keepdims=True)
    hp = (x1 * lax.rsqrt(ms + EPS) * gp_ref[...]).astype(BF16)
    ple = _dot(p_ref[...].astype(BF16), wp_ref[...])
    out_ref[...] = x1 + jax.nn.sigmoid(_dot(hp, wg_ref[...])) * ple


def _merge(z, o_b, o_c, x, p, seq_len, cw, cb, gv, ws, bs, wup, mb, wo, gp, wg, wp):
    t = x.shape[0]
    tm = min(256, seq_len)
    w = BRANCH_WIDTH
    hstep = tm // HALO
    row = lambda width, c: pl.BlockSpec((tm, width), lambda i, c=c: (i, c))
    halo = lambda c: pl.BlockSpec((HALO, w), lambda i, c=c: (jnp.maximum(i * hstep - 1, 0), c))
    full = lambda a: pl.BlockSpec(a.shape, lambda i, n=a.ndim: (0,) * n)
    consts = (cw, cb, gv, ws, bs, wup, mb, wo, gp, wg, wp)
    return pl.pallas_call(
        functools.partial(_merge_kernel, tiles_per_seq=seq_len // tm),
        out_shape=jax.ShapeDtypeStruct((t, D_MODEL), F32),
        grid=(t // tm,),
        in_specs=[
            row(N_BRANCH * D_MODEL, COL_MERGE),
            row(4 * w, COL_A // 4),
            row(w, COL_D), row(w, COL_D + 1), row(w, COL_D + 2),
            row(w, COL_B + 3), row(w, COL_C + 3),
            halo(COL_A), halo(COL_A + 2),
            row(w, 0), row(w, 0),
            row(D_MODEL, 0), row(PLE_DIM, 0),
        ] + [full(a) for a in consts],
        out_specs=pl.BlockSpec((tm, D_MODEL), lambda i: (i, 0)),
        compiler_params=_cparams(("parallel",)),
        name="merge",
    )(z, z, z, z, z, z, z, z, z, o_b, o_c, x, p, *consts)


def _reorder_w_in(w_in):
    w = BRANCH_WIDTH
    n_pre = 8 * w
    n_f = N_HEADS
    n_cd = 7 * w
    pre = w_in[:, :, :n_pre]
    fcols = w_in[:, :, n_pre:n_pre + n_f]
    cd = w_in[:, :, n_pre + n_f:n_pre + n_f + n_cd]
    merge = w_in[:, :, n_pre + n_f + n_cd:]
    pad = jnp.zeros(w_in.shape[:2] + (w - n_f,), w_in.dtype)
    return jnp.concatenate([merge, pre, cd, fcols, pad], axis=-1).astype(BF16)


def kernel(x, p, norm_mix, w_in, conv_w, conv_b, fgate_bias, q_norm, k_norm, lb_logits,
           hgrn_norm, sgu_norm, spatial_w, spatial_b, w_up, merge_b, w_o, norm_ple,
           w_ple_gate, w_ple_proj):
    bn, s, _ = x.shape
    depth = w_in.shape[0]
    t = bn * s
    assert w_in.shape[-1] == 15 * BRANCH_WIDTH + N_HEADS + N_BRANCH * D_MODEL
    assert s % SPATIAL_CHUNK == 0 and s % GLA_CHUNK == 0

    lb_p = jax.nn.softmax(lb_logits.astype(F32), axis=0)
    lower_bounds = jnp.clip(jnp.cumsum(lb_p, axis=0) - lb_p[0], 0.0, 1.0)
    w_z = _reorder_w_in(w_in)
    fbias = jnp.pad(fgate_bias.astype(F32), ((0, 0), (0, AUG - N_HEADS)))
    gq = jnp.pad(q_norm.astype(F32) * (HEAD_DIM ** -0.5 * LOG2E), ((0, 0), (0, AUG - HEAD_DIM)))
    gk = jnp.pad(k_norm.astype(F32), ((0, 0), (0, AUG - HEAD_DIM)))
    qk_bound = (HEAD_DIM * BF16_NORM_MARGIN) * jnp.max(jnp.abs(gq), axis=1) * jnp.max(jnp.abs(gk), axis=1)
    skip_slack = 2.0 * qk_bound + ATTN_CUTOFF_LOG2
    causal = jnp.tril(jnp.ones((SPATIAL_CHUNK, SPATIAL_CHUNK), F32))
    ws = (spatial_w.astype(F32) * causal).transpose(0, 2, 1, 3).reshape(
        depth, SPATIAL_CHUNK, N_HEADS * SPATIAL_CHUNK).astype(BF16)
    bs = jnp.repeat(spatial_b.astype(F32).transpose(0, 2, 1), HEAD_DIM, axis=-1)
    w_up_b = w_up.astype(BF16)
    w_o_b = w_o.astype(BF16)
    w_g_b = w_ple_gate.astype(BF16)
    w_p_b = w_ple_proj.astype(BF16)

    xf = x.reshape(t, D_MODEL)
    for li in range(depth):
        z = _in_proj(xf, norm_mix[li][None], w_z[li])
        z3 = z.reshape(bn, s, Z_COLS)
        qa, ka, va, edge = _attn_prep(z3, fbias[li][None], gq[li][None], gk[li][None])
        cfirst = edge[:, :, 0, :N_HEADS].transpose(0, 2, 1).reshape(-1) + skip_slack[li]
        clast = edge[:, :, 1, :N_HEADS].transpose(0, 2, 1).reshape(-1)
        o_b = _fox_attn(qa, ka, va, cfirst, clast).reshape(t, BRANCH_WIDTH)
        o_c = _hgrn(z3, lower_bounds[li][None], hgrn_norm[li][None]).reshape(t, BRANCH_WIDTH)
        xf = _merge(z, o_b, o_c, xf, p[li].reshape(t, PLE_DIM), s,
                    conv_w[li], conv_b[li][None], sgu_norm[li][None], ws[li], bs[li],
                    w_up_b[li], merge_b[li], w_o_b[li], norm_ple[li][None], w_g_b[li], w_p_b[li])
    return xf.reshape(bn, s, D_MODEL).astype(x.dtype)
```

```python
import functools
import math

import jax
import jax.numpy as jnp
import numpy as np
from jax import lax
from jax.experimental import pallas as pl
from jax.experimental.pallas import tpu as pltpu

F32 = jnp.float32
BF16 = jnp.bfloat16

D_MODEL = 1024
PLE_DIM = 256
N_BRANCH = 4
BRANCH_WIDTH = 256
HEAD_DIM = 64
N_HEADS = BRANCH_WIDTH // HEAD_DIM
CONV_WIDTH = 3
GLA_CHUNK = 64
SPATIAL_CHUNK = 128
EPS = 1e-6
MASK_VALUE = -1e30
LOG2E = math.log2(math.e)

Z_COLS = 8192
COL_MERGE = 0
COL_A = 16
COL_B = 20
COL_C = 24
COL_D = 28
COL_F = 31

AUG = 128
LANE_C = HEAD_DIM
IN_PROJ_ROWS = 1024
IN_PROJ_COLS = 2048
MERGE_ROWS = 512
ATTN_BLOCK = 512
ATTN_STREAMS = 4
ATTN_WIDE = 2
ATTN_CUTOFF_LOG2 = 152.0
BF16_NORM_MARGIN = 1.02

VMEM_LIMIT_BYTES = 56 * 1024 * 1024
HGRN_SAFE_RANGE = 60.0


def _attn_block(seq_len):
    return min(ATTN_BLOCK, seq_len)


def _cparams(semantics):
    return pltpu.CompilerParams(dimension_semantics=semantics, vmem_limit_bytes=VMEM_LIMIT_BYTES)


def _split3(x):
    a = x.astype(BF16)
    r = x - a.astype(F32)
    b = r.astype(BF16)
    c = (r - b.astype(F32)).astype(BF16)
    return a, b, c


def _pack3(x):
    a, b, c = (piece.astype(F32) for piece in _split3(x))
    lane = _iota(x.shape, 1)
    packed = jnp.where(
        lane < N_HEADS, a,
        jnp.where(lane < 2 * N_HEADS, pltpu.roll(b, N_HEADS, axis=1),
                  jnp.where(lane < 3 * N_HEADS, pltpu.roll(c, 2 * N_HEADS, axis=1), 0.0)))
    return packed.astype(BF16)


def _dot(a, b):
    return jnp.dot(a, b, preferred_element_type=F32)


def _dot_nt(a, b):
    return lax.dot_general(a, b, (((1,), (1,)), ((), ())), preferred_element_type=F32)


def _dot_tn(a, b):
    return lax.dot_general(a, b, (((0,), (0,)), ((), ())), preferred_element_type=F32)


def _iota(shape, dim):
    return lax.broadcasted_iota(jnp.int32, shape, dim)


def _head_block_mask(rows, cols):
    return (_iota((rows, cols), 0) // HEAD_DIM) == (_iota((rows, cols), 1) // HEAD_DIM)


def _group_mean_sq(x):
    w = x.shape[-1]
    avg = jnp.where(_head_block_mask(w, w), 1.0 / HEAD_DIM, 0.0).astype(BF16)
    sq = x * x
    hi = sq.astype(BF16)
    lo = (sq - hi.astype(F32)).astype(BF16)
    return _dot(hi, avg) + _dot(lo, avg)


def _in_proj_kernel(x_ref, g_ref, w_ref, z_ref, h_ref):
    @pl.when(pl.program_id(1) == 0)
    def _():
        x = x_ref[...]
        ms = jnp.mean(x * x, axis=-1, keepdims=True)
        h_ref[...] = (x * lax.rsqrt(ms + EPS) * g_ref[...]).astype(BF16)

    z_ref[...] = _dot(h_ref[...], w_ref[...]).astype(z_ref.dtype)


def _in_proj(x, g, w_all, li):
    t = x.shape[0]
    tm = min(IN_PROJ_ROWS, t)
    tn = IN_PROJ_COLS
    return pl.pallas_call(
        _in_proj_kernel,
        out_shape=jax.ShapeDtypeStruct((t, Z_COLS), BF16),
        grid=(t // tm, Z_COLS // tn),
        in_specs=[
            pl.BlockSpec((tm, D_MODEL), lambda i, j: (i, 0)),
            pl.BlockSpec((1, D_MODEL), lambda i, j: (0, 0)),
            pl.BlockSpec((None, D_MODEL, tn), lambda i, j: (li, 0, j)),
        ],
        out_specs=pl.BlockSpec((tm, tn), lambda i, j: (i, j)),
        scratch_shapes=[pltpu.VMEM((tm, D_MODEL), BF16)],
        compiler_params=_cparams(("parallel", "arbitrary")),
        name="in_proj",
    )(x, g, w_all)


def _attn_prep_kernel(q_ref, k_ref, v_ref, f_ref, fb_ref, gq_ref, gk_ref,
                      wq_ref, wk_ref, wv_ref, oq_ref, ok_ref, ov_ref,
                      qa_ref, ka_ref, va_ref, edge_ref, carry_ref):
    ts = q_ref.shape[1]

    @pl.when(pl.program_id(1) == 0)
    def _():
        carry_ref[...] = jnp.zeros_like(carry_ref)

    f = f_ref[0][:, :AUG].astype(F32) + fb_ref[...]
    log_f = (jnp.minimum(f, 0.0) - jnp.log1p(jnp.exp(-jnp.abs(f)))) * LOG2E
    tril = (_iota((ts, ts), 0) >= _iota((ts, ts), 1)).astype(BF16)
    part = _dot(tril, _pack3(log_f))
    total = (part + pltpu.roll(part, AUG - N_HEADS, axis=1)
             + pltpu.roll(part, AUG - 2 * N_HEADS, axis=1))
    cum = jnp.where(_iota((ts, AUG), 1) < N_HEADS, total, 0.0) + carry_ref[...]
    carry_ref[...] = cum[ts - 1:ts, :]
    edge_row = _iota((8, AUG), 0)
    edge_ref[0, 0] = jnp.where(edge_row == 0, cum[0:1, :],
                               jnp.where(edge_row == 1, cum[ts - 1:ts, :], 0.0))

    def normed(ref, gain_ref):
        x = ref[0].astype(F32)
        return (x * lax.rsqrt(_group_mean_sq(x) + EPS) * gain_ref[...]).astype(BF16)

    ccat = _pack3(cum)
    qa = _dot(jnp.concatenate([normed(q_ref, gq_ref), ccat], axis=1), wq_ref[...]) + oq_ref[...]
    ka = _dot(jnp.concatenate([normed(k_ref, gk_ref), ccat], axis=1), wk_ref[...]) + ok_ref[...]
    va = _dot(v_ref[0], wv_ref[...]) + ov_ref[...]
    for h in range(N_HEADS):
        lanes = slice(h * AUG, (h + 1) * AUG)
        qa_ref[0, h] = qa[:, lanes].astype(BF16)
        ka_ref[0, h] = ka[:, lanes].astype(BF16)
        va_ref[0, h] = va[:, lanes].astype(BF16)


def _attn_layout_constants():
    w, wide = BRANCH_WIDTH, N_HEADS * AUG
    place = np.zeros((w, wide), np.float32)
    sel_q = np.zeros((AUG, wide), np.float32)
    sel_k = np.zeros((AUG, wide), np.float32)
    ones_q = np.zeros((1, wide), np.float32)
    ones_k = np.zeros((1, wide), np.float32)
    ones_v = np.zeros((1, wide), np.float32)
    for h in range(N_HEADS):
        for d in range(HEAD_DIM):
            place[h * HEAD_DIM + d, h * AUG + d] = 1.0
        for i in range(3):
            sel_q[i * N_HEADS + h, h * AUG + LANE_C + i] = 1.0
            sel_k[i * N_HEADS + h, h * AUG + LANE_C + 3 + i] = -1.0
            ones_k[0, h * AUG + LANE_C + i] = 1.0
            ones_q[0, h * AUG + LANE_C + 3 + i] = 1.0
        ones_v[0, h * AUG + HEAD_DIM] = 1.0
    bf = lambda a: jnp.asarray(a, BF16)
    return (bf(np.concatenate([place, sel_q])), bf(np.concatenate([place, sel_k])), bf(place),
            jnp.asarray(ones_q), jnp.asarray(ones_k), jnp.asarray(ones_v))


def _attn_prep(z3, fbias, gq, gk):
    b, s, _ = z3.shape
    ts = _attn_block(s)
    col = lambda c: pl.BlockSpec((1, ts, BRANCH_WIDTH), lambda bi, si, c=c: (bi, si, c))
    full = lambda a: pl.BlockSpec(a.shape, lambda bi, si: (0, 0))
    aug = pl.BlockSpec((1, N_HEADS, ts, AUG), lambda bi, si: (bi, 0, si, 0))
    shape = jax.ShapeDtypeStruct((b, N_HEADS, s, AUG), BF16)
    edge_shape = jax.ShapeDtypeStruct((b, s // ts, 8, AUG), F32)
    consts = (fbias, gq, gk) + _attn_layout_constants()
    return pl.pallas_call(
        _attn_prep_kernel,
        out_shape=(shape, shape, shape, edge_shape),
        grid=(b, s // ts),
        in_specs=[col(COL_B), col(COL_B + 1), col(COL_B + 2), col(COL_F)] + [full(a) for a in consts],
        out_specs=(aug, aug, aug, pl.BlockSpec((1, 1, 8, AUG), lambda bi, si: (bi, si, 0, 0))),
        scratch_shapes=[pltpu.VMEM((1, AUG), F32)],
        compiler_params=_cparams(("parallel", "arbitrary")),
        name="attn_prep",
    )(z3, z3, z3, z3, *consts)


def _fox_attn_kernel(cfirst_ref, clast_ref, qa_ref, ka_ref, va_ref, o_ref, m_ref, acc_ref):
    bi = pl.program_id(0)
    qi = pl.program_id(1)
    nblk = pl.num_programs(1)
    tq = qa_ref.shape[2]
    tr = tq // ATTN_STREAMS

    def process(h, kj, nblocks, diagonal):
        start = pl.multiple_of(kj * tq, tq)
        widths = [(nblocks - 1) * tq + (r + 1) * tr if diagonal else nblocks * tq
                  for r in range(ATTN_STREAMS)]
        scores = [
            _dot_nt(qa_ref[0, h, r * tr:(r + 1) * tr, :],
                    ka_ref[0, h, pl.ds(start, widths[r]), :])
            for r in range(ATTN_STREAMS)]
        for r, s in enumerate(scores):
            wk = widths[r]
            if diagonal:
                edge = s[:, wk - tr:]
                edge = jnp.where(_iota((tr, tr), 1) <= _iota((tr, tr), 0), edge, MASK_VALUE)
                s = edge if wk == tr else jnp.concatenate([s[:, :wk - tr], edge], axis=1)
            m_prev = m_ref[h, r]
            m_new = jnp.maximum(m_prev, jnp.max(s, axis=-1, keepdims=True))
            alpha = jnp.exp2(m_prev - m_new)
            p = jnp.concatenate(
                [jnp.exp2(s[:, c * AUG:(c + 1) * AUG] - m_new) for c in range(wk // AUG)], axis=1)
            acc_ref[h, r] = acc_ref[h, r] * alpha + _dot(
                p.astype(BF16), va_ref[0, h, pl.ds(start, wk), :])
            m_ref[h, r] = m_new

    m_ref[...] = jnp.full_like(m_ref, MASK_VALUE)
    acc_ref[...] = jnp.zeros_like(acc_ref)

    last_off = jnp.maximum(qi - 1, 0)
    for h in range(N_HEADS):
        base = (bi * N_HEADS + h) * nblk
        c_i = cfirst_ref[base + qi]
        j_lo = lax.while_loop(
            lambda j: (j > 0) & (c_i - clast_ref[base + jnp.maximum(j - 1, 0)] >= 0.0),
            lambda j: j - 1, last_off)
        n_wide = (last_off - j_lo) // ATTN_WIDE

        def wide_body(t, carry, h=h, j_lo=j_lo):
            process(h, j_lo + t * ATTN_WIDE, ATTN_WIDE, False)
            return carry

        def single_body(j, carry, h=h):
            process(h, j, 1, False)
            return carry

        lax.fori_loop(0, n_wide, wide_body, 0)
        lax.fori_loop(j_lo + n_wide * ATTN_WIDE, last_off, single_body, 0)

    @pl.when(qi > 0)
    def _():
        for h in range(N_HEADS):
            process(h, qi - 1, 2, True)

    @pl.when(qi == 0)
    def _():
        for h in range(N_HEADS):
            process(h, qi, 1, True)

    outs = []
    for h in range(N_HEADS):
        acc = acc_ref[h].reshape(tq, AUG)
        outs.append(acc[:, :HEAD_DIM] / acc[:, HEAD_DIM:HEAD_DIM + 1])
    o_ref[0] = jnp.concatenate(outs, axis=-1)


def _fox_attn(qa, ka, va, cfirst, clast):
    b, _, s, _ = qa.shape
    tq = _attn_block(s)
    q_spec = pl.BlockSpec((1, N_HEADS, tq, AUG), lambda bi, i, cf, cl: (bi, 0, i, 0))
    kv_spec = pl.BlockSpec((1, N_HEADS, s, AUG), lambda bi, i, cf, cl: (bi, 0, 0, 0))
    return pl.pallas_call(
        _fox_attn_kernel,
        out_shape=jax.ShapeDtypeStruct((b, s, BRANCH_WIDTH), F32),
        grid_spec=pltpu.PrefetchScalarGridSpec(
            num_scalar_prefetch=2,
            grid=(b, s // tq),
            in_specs=[q_spec, kv_spec, kv_spec],
            out_specs=pl.BlockSpec((1, tq, BRANCH_WIDTH), lambda bi, i, cf, cl: (bi, i, 0)),
            scratch_shapes=[pltpu.VMEM((N_HEADS, ATTN_STREAMS, tq // ATTN_STREAMS, AUG), F32),
                            pltpu.VMEM((N_HEADS, ATTN_STREAMS, tq // ATTN_STREAMS, AUG), F32)],
        ),
        compiler_params=_cparams(("parallel", "arbitrary")),
        name="fox_attn",
    )(cfirst, clast, qa, ka, va)


def _hgrn_kernel(q_ref, f_ref, i_ref, lb_ref, gain_ref, o_ref,
                 state_ref, qs_ref, ks_ref, vs_ref, lg_ref, oc_ref):
    tc = q_ref.shape[1]
    c = GLA_CHUNK
    half = c // 2
    w = BRANCH_WIDTH

    @pl.when(pl.program_id(1) == 0)
    def _():
        state_ref[...] = jnp.zeros_like(state_ref)

    n = tc // c
    lb = lb_ref[...]
    bd_mask = _head_block_mask(w, w)

    q = jax.nn.silu(q_ref[0].astype(F32))
    fl = f_ref[0].astype(F32)
    v = i_ref[0].astype(F32)
    t_small = jnp.exp(-jnp.abs(fl))
    s_big = 1.0 / (1.0 + t_small)
    s_small = t_small * s_big
    pos = fl >= 0.0
    log_g = jnp.log(lb + (1.0 - lb) * jnp.where(pos, s_big, s_small))
    kf = (1.0 - lb) * jnp.where(pos, s_small, s_big)

    r_i, c_i = _iota((tc, tc), 0), _iota((tc, tc), 1)
    tril = ((r_i >= c_i) & (r_i // c == c_i // c)).astype(BF16)
    l1, l2, l3 = _split3(log_g)
    bsum = _dot(tril, l1) + _dot(tril, l2) + _dot(tril, l3)

    b3 = bsum.reshape(n, c, w)
    b_first, b_q1 = b3[:, 0:1], b3[:, half // 2 - 1:half // 2]
    b_mid, b_mid1 = b3[:, half - 1:half], b3[:, half:half + 1]
    b_q3, b_last = b3[:, half + half // 2 - 1:half + half // 2], b3[:, c - 1:c]
    spread = jnp.maximum(jnp.maximum(b_first - b_q1, b_q1 - b_mid),
                         jnp.maximum(b_mid1 - b_q3, b_q3 - b_last))
    safe = jnp.max(spread) <= HGRN_SAFE_RANGE

    @pl.when(safe)
    def _():
        top = _iota((n, c, w), 1) < half
        ref_d = jnp.where(top, b_q1, b_q3)
        q3, k3 = q.reshape(n, c, w), kf.reshape(n, c, w)
        qd = q3 * jnp.exp(b3 - ref_d)
        kd = k3 * jnp.exp(ref_d - b3)
        qo = jnp.where(top, 0.0, q3 * jnp.exp(jnp.minimum(b3 - b_mid, 0.0)))
        ko = jnp.where(top, k3 * jnp.exp(jnp.minimum(b_mid - b3, 0.0)), 0.0)
        qcat = jnp.concatenate(
            [qo, jnp.where(top, qd, 0.0), jnp.where(top, 0.0, qd)], axis=2).astype(BF16)
        kparts = [x.astype(BF16) for x in (ko, jnp.where(top, kd, 0.0), jnp.where(top, 0.0, kd))]
        q_in = (q3 * jnp.exp(b3)).astype(BF16)
        k_end = (k3 * jnp.exp(b_last - b3)).astype(BF16)
        e_last = jnp.exp(b_last)
        v_b = v.astype(BF16).reshape(n, c, w)
        causal = (_iota((c, w), 1) % HEAD_DIM) <= _iota((c, w), 0)

        def block_diag(x):
            return jnp.where(bd_mask, jnp.concatenate([x] * N_HEADS, axis=0), 0.0)

        scores = []
        for ci in range(n):
            kcat = jnp.concatenate([block_diag(kp[ci]) for kp in kparts], axis=1)
            scores.append(jnp.where(causal, _dot_nt(qcat[ci], kcat), 0.0).astype(BF16))
        updates = [jnp.where(bd_mask, _dot_tn(v_b[ci], k_end[ci]), 0.0) for ci in range(n)]
        o_intra = [_dot(scores[ci], block_diag(v_b[ci])) for ci in range(n)]
        state_t = state_ref[...]
        outs = []
        for ci in range(n):
            outs.append(o_intra[ci] + _dot_nt(q_in[ci], state_t.astype(BF16)))
            state_t = state_t * e_last[ci] + updates[ci]
        state_ref[...] = state_t
        oc_ref[...] = jnp.concatenate(outs, axis=0)

    @pl.when(jnp.logical_not(safe))
    def _():
        qs_ref[...] = q
        ks_ref[...] = kf
        vs_ref[...] = v
        lg_ref[...] = log_g

        def body(t, carry):
            g_t = jnp.exp(lg_ref[pl.ds(t, 1), :])
            outer = _dot_tn(vs_ref[pl.ds(t, 1), :].astype(BF16),
                            ks_ref[pl.ds(t, 1), :].astype(BF16))
            st = state_ref[...] * g_t + jnp.where(bd_mask, outer, 0.0)
            state_ref[...] = st
            oc_ref[pl.ds(t, 1), :] = _dot_nt(qs_ref[pl.ds(t, 1), :].astype(BF16), st.astype(BF16))
            return carry

        lax.fori_loop(0, tc, body, 0)

    o = oc_ref[...]
    o_ref[0] = o * lax.rsqrt(_group_mean_sq(o) + EPS) * gain_ref[...]


def _hgrn(z3, lb, gain):
    b, s, _ = z3.shape
    tc = min(512, s)
    col = lambda c: pl.BlockSpec((1, tc, BRANCH_WIDTH), lambda bi, si, c=c: (bi, si, c))
    vec = pl.BlockSpec((1, BRANCH_WIDTH), lambda bi, si: (0, 0))
    tile = pltpu.VMEM((tc, BRANCH_WIDTH), F32)
    return pl.pallas_call(
        _hgrn_kernel,
        out_shape=jax.ShapeDtypeStruct((b, s, BRANCH_WIDTH), F32),
        grid=(b, s // tc),
        in_specs=[col(COL_C), col(COL_C + 1), col(COL_C + 2), vec, vec],
        out_specs=pl.BlockSpec((1, tc, BRANCH_WIDTH), lambda bi, si: (bi, si, 0)),
        scratch_shapes=[pltpu.VMEM((BRANCH_WIDTH, BRANCH_WIDTH), F32),
                        tile, tile, tile, tile, tile],
        compiler_params=_cparams(("parallel", "arbitrary")),
        name="hgrn2",
    )(z3, z3, z3, lb, gain)


HALO = 16


def _merge_kernel(zm_ref, za_ref, du_ref, dv_ref, dg_ref, bg_ref, cg_ref, hx_ref, hc_ref, ob_ref, oc_ref,
                  x_ref, p_ref, cw_ref, cb_ref, gv_ref, ws_ref, bs_ref, wup_ref, mb_ref,
                  wo_ref, gp_ref, wg_ref, wp_ref, out_ref, *, tiles_per_seq):
    tm = x_ref.shape[0]
    w = BRANCH_WIDTH

    za = za_ref[...].astype(F32)
    zc = za[:, 2 * w:3 * w] * za[:, 0:w]
    halo = hc_ref[...].astype(F32) * hx_ref[...].astype(F32)
    halo = jnp.where(pl.program_id(0) % tiles_per_seq == 0, 0.0, halo)
    ext = jnp.concatenate([halo, zc], axis=0)
    conv = zc * cw_ref[2:3, :]
    for tap in range(CONV_WIDTH - 1):
        shift = CONV_WIDTH - 1 - tap
        conv = conv + pltpu.roll(ext, shift, axis=0)[HALO:] * cw_ref[tap:tap + 1, :]
    y_a = za[:, w:2 * w] * (conv + cb_ref[...]) * jax.nn.silu(za[:, 3 * w:4 * w])

    y_b = ob_ref[...] * jax.nn.silu(bg_ref[...].astype(F32))
    y_c = oc_ref[...] * jax.nn.silu(cg_ref[...].astype(F32))

    vd = dv_ref[...].astype(F32)
    vn = (vd * lax.rsqrt(_group_mean_sq(vd) + EPS) * gv_ref[...])
    lane_head = _iota((SPATIAL_CHUNK, w), 1) // HEAD_DIM
    mixed = []
    for ci in range(tm // SPATIAL_CHUNK):
        vc = vn[ci * SPATIAL_CHUNK:(ci + 1) * SPATIAL_CHUNK]
        stacked = jnp.concatenate(
            [jnp.where(lane_head == g, vc, 0.0) for g in range(N_HEADS)], axis=0).astype(BF16)
        mixed.append(_dot(ws_ref[...], stacked) + bs_ref[...])
    y_d = (du_ref[...].astype(F32) * jnp.concatenate(mixed, axis=0)
           * jax.nn.silu(dg_ref[...].astype(F32)))

    merged = None
    for bi, y in enumerate((y_a, y_b, y_c, y_d)):
        gate = jax.nn.sigmoid(
            zm_ref[:, bi * D_MODEL:(bi + 1) * D_MODEL].astype(F32) + mb_ref[bi:bi + 1, :])
        term = gate * _dot(y.astype(BF16), wup_ref[bi])
        merged = term if merged is None else merged + term
    x1 = x_ref[...] + _dot(merged.astype(BF16), wo_ref[...])

    ms = jnp.mean(x1 * x1, axis=-1, keepdims=True)
    hp = (x1 * lax.rsqrt(ms + EPS) * gp_ref[...]).astype(BF16)
    ple = _dot(p_ref[...].astype(BF16), wp_ref[...])
    out_ref[...] = x1 + jax.nn.sigmoid(_dot(hp, wg_ref[...])) * ple


def _merge(z, o_b, o_c, x, p, seq_len, li, cw, cb, gv, ws, bs, wup, mb, wo, gp, wg, wp):
    t = x.shape[0]
    tm = min(MERGE_ROWS, seq_len)
    w = BRANCH_WIDTH
    hstep = tm // HALO
    row = lambda width, c: pl.BlockSpec((tm, width), lambda i, c=c: (i, c))
    halo = lambda c: pl.BlockSpec((HALO, w), lambda i, c=c: (jnp.maximum(i * hstep - 1, 0), c))

    stacked = {id(a) for a in (ws, bs, wup, wo, wg, wp)}

    def full(a):
        if id(a) in stacked:
            return pl.BlockSpec((None,) + a.shape[1:], lambda i, n=a.ndim: (li,) + (0,) * (n - 1))
        return pl.BlockSpec(a.shape, lambda i, n=a.ndim: (0,) * n)

    consts = (cw, cb, gv, ws, bs, wup, mb, wo, gp, wg, wp)
    return pl.pallas_call(
        functools.partial(_merge_kernel, tiles_per_seq=seq_len // tm),
        out_shape=jax.ShapeDtypeStruct((t, D_MODEL), F32),
        grid=(t // tm,),
        in_specs=[
            row(N_BRANCH * D_MODEL, COL_MERGE),
            row(4 * w, COL_A // 4),
            row(w, COL_D), row(w, COL_D + 1), row(w, COL_D + 2),
            row(w, COL_B + 3), row(w, COL_C + 3),
            halo(COL_A), halo(COL_A + 2),
            row(w, 0), row(w, 0),
            row(D_MODEL, 0),
            pl.BlockSpec((None, tm, PLE_DIM), lambda i: (li, i, 0)),
        ] + [full(a) for a in consts],
        out_specs=pl.BlockSpec((tm, D_MODEL), lambda i: (i, 0)),
        compiler_params=_cparams(("parallel",)),
        name="merge",
    )(z, z, z, z, z, z, z, z, z, o_b, o_c, x, p, *consts)


def _reorder_w_in(w_in):
    w = BRANCH_WIDTH
    n_pre = 8 * w
    n_f = N_HEADS
    n_cd = 7 * w
    pre = w_in[:, :, :n_pre]
    fcols = w_in[:, :, n_pre:n_pre + n_f]
    cd = w_in[:, :, n_pre + n_f:n_pre + n_f + n_cd]
    merge = w_in[:, :, n_pre + n_f + n_cd:]
    pad = jnp.zeros(w_in.shape[:2] + (w - n_f,), w_in.dtype)
    return jnp.concatenate([merge, pre, cd, fcols, pad], axis=-1).astype(BF16)


def kernel(x, p, norm_mix, w_in, conv_w, conv_b, fgate_bias, q_norm, k_norm, lb_logits,
           hgrn_norm, sgu_norm, spatial_w, spatial_b, w_up, merge_b, w_o, norm_ple,
           w_ple_gate, w_ple_proj):
    bn, s, _ = x.shape
    depth = w_in.shape[0]
    t = bn * s
    assert w_in.shape[-1] == 15 * BRANCH_WIDTH + N_HEADS + N_BRANCH * D_MODEL
    assert s % SPATIAL_CHUNK == 0 and s % GLA_CHUNK == 0

    lb_p = jax.nn.softmax(lb_logits.astype(F32), axis=0)
    lower_bounds = jnp.clip(jnp.cumsum(lb_p, axis=0) - lb_p[0], 0.0, 1.0)
    w_z = _reorder_w_in(w_in)
    fbias = jnp.pad(fgate_bias.astype(F32), ((0, 0), (0, AUG - N_HEADS)))
    gq = jnp.tile(q_norm.astype(F32) * (HEAD_DIM ** -0.5 * LOG2E), (1, N_HEADS))
    gk = jnp.tile(k_norm.astype(F32), (1, N_HEADS))
    qk_bound = (HEAD_DIM * BF16_NORM_MARGIN) * jnp.max(jnp.abs(gq), axis=1) * jnp.max(jnp.abs(gk), axis=1)
    skip_slack = 2.0 * qk_bound + ATTN_CUTOFF_LOG2
    causal = jnp.tril(jnp.ones((SPATIAL_CHUNK, SPATIAL_CHUNK), F32))
    ws = (spatial_w.astype(F32) * causal).transpose(0, 2, 1, 3).reshape(
        depth, SPATIAL_CHUNK, N_HEADS * SPATIAL_CHUNK).astype(BF16)
    bs = jnp.repeat(spatial_b.astype(F32).transpose(0, 2, 1), HEAD_DIM, axis=-1)
    w_up_b = w_up.astype(BF16)
    w_o_b = w_o.astype(BF16)
    w_g_b = w_ple_gate.astype(BF16)
    w_p_b = w_ple_proj.astype(BF16)

    xf = x.reshape(t, D_MODEL)
    p_rows = p.reshape(depth, t, PLE_DIM)
    for li in range(depth):
        z = _in_proj(xf, norm_mix[li][None], w_z, li)
        z3 = z.reshape(bn, s, Z_COLS)
        qa, ka, va, edge = _attn_prep(z3, fbias[li][None], gq[li][None], gk[li][None])
        cfirst = edge[:, :, 0, :N_HEADS].transpose(0, 2, 1).reshape(-1) + skip_slack[li]
        clast = edge[:, :, 1, :N_HEADS].transpose(0, 2, 1).reshape(-1)
        o_b = _fox_attn(qa, ka, va, cfirst, clast).reshape(t, BRANCH_WIDTH)
        o_c = _hgrn(z3, lower_bounds[li][None], hgrn_norm[li][None]).reshape(t, BRANCH_WIDTH)
        xf = _merge(z, o_b, o_c, xf, p_rows, s, li,
                    conv_w[li], conv_b[li][None], sgu_norm[li][None], ws, bs,
                    w_up_b, merge_b[li], w_o_b, norm_ple[li][None], w_g_b, w_p_b)
    return xf.reshape(bn, s, D_MODEL).astype(x.dtype)
```

```python
import functools
import math

import jax
import jax.numpy as jnp
import numpy as np
from jax import lax
from jax.experimental import pallas as pl
from jax.experimental.pallas import tpu as pltpu

F32 = jnp.float32
BF16 = jnp.bfloat16

D_MODEL = 1024
PLE_DIM = 256
N_BRANCH = 4
BRANCH_WIDTH = 256
HEAD_DIM = 64
N_HEADS = BRANCH_WIDTH // HEAD_DIM
CONV_WIDTH = 3
GLA_CHUNK = 64
SPATIAL_CHUNK = 128
EPS = 1e-6
MASK_VALUE = -1e30
LOG2E = math.log2(math.e)

Z_COLS = 8192
COL_MERGE = 0
COL_A = 16
COL_B = 20
COL_C = 24
COL_D = 28
COL_F = 31
SILU_COLS = (COL_A + 3, COL_B + 3, COL_C, COL_C + 3, COL_D + 2)

AUG = 128
LANE_C = HEAD_DIM
IN_PROJ_ROWS = 1024
IN_PROJ_COLS = 2048
MERGE_ROWS = 512
ATTN_BLOCK = 512
ATTN_STREAMS = 4
ATTN_WIDE = 2
ATTN_CUTOFF_LOG2 = 152.0
BF16_NORM_MARGIN = 1.02

VMEM_LIMIT_BYTES = 56 * 1024 * 1024
HGRN_SAFE_RANGE = 60.0


def _attn_block(seq_len):
    return min(ATTN_BLOCK, seq_len)


def _cparams(semantics):
    return pltpu.CompilerParams(dimension_semantics=semantics, vmem_limit_bytes=VMEM_LIMIT_BYTES)


def _split3(x):
    a = x.astype(BF16)
    r = x - a.astype(F32)
    b = r.astype(BF16)
    c = (r - b.astype(F32)).astype(BF16)
    return a, b, c


def _pack3(x):
    a, b, c = (piece.astype(F32) for piece in _split3(x))
    lane = _iota(x.shape, 1)
    packed = jnp.where(
        lane < N_HEADS, a,
        jnp.where(lane < 2 * N_HEADS, pltpu.roll(b, N_HEADS, axis=1),
                  jnp.where(lane < 3 * N_HEADS, pltpu.roll(c, 2 * N_HEADS, axis=1), 0.0)))
    return packed.astype(BF16)


def _dot(a, b):
    return jnp.dot(a, b, preferred_element_type=F32)


def _dot_nt(a, b):
    return lax.dot_general(a, b, (((1,), (1,)), ((), ())), preferred_element_type=F32)


def _dot_tn(a, b):
    return lax.dot_general(a, b, (((0,), (0,)), ((), ())), preferred_element_type=F32)


def _iota(shape, dim):
    return lax.broadcasted_iota(jnp.int32, shape, dim)


def _head_block_mask(rows, cols):
    return (_iota((rows, cols), 0) // HEAD_DIM) == (_iota((rows, cols), 1) // HEAD_DIM)


def _group_mean_sq(x):
    w = x.shape[-1]
    avg = jnp.where(_head_block_mask(w, w), 1.0 / HEAD_DIM, 0.0).astype(BF16)
    sq = x * x
    hi = sq.astype(BF16)
    lo = (sq - hi.astype(F32)).astype(BF16)
    return _dot(hi, avg) + _dot(lo, avg)


def _in_proj_kernel(x_ref, g_ref, w_ref, mb_ref, z_ref, h_ref):
    j = pl.program_id(1)
    tn = z_ref.shape[1]
    blocks_per_step = tn // BRANCH_WIDTH

    @pl.when(j == 0)
    def _():
        x = x_ref[...]
        ms = jnp.mean(x * x, axis=-1, keepdims=True)
        h_ref[...] = (x * lax.rsqrt(ms + EPS) * g_ref[...]).astype(BF16)

    n_merge_steps = (COL_A - COL_MERGE) // blocks_per_step

    @pl.when(j < n_merge_steps)
    def _():
        res = _dot(h_ref[...], w_ref[...])
        gate = 0.5 * jnp.tanh(0.5 * (res + mb_ref[...])) + 0.5
        z_ref[...] = gate.astype(z_ref.dtype)

    for step in range(n_merge_steps, Z_COLS // tn):
        @pl.when(j == step)
        def _(step=step):
            res = _dot(h_ref[...], w_ref[...])
            for blk in range(blocks_per_step):
                piece = res[:, blk * BRANCH_WIDTH:(blk + 1) * BRANCH_WIDTH]
                if step * blocks_per_step + blk in SILU_COLS:
                    piece = jax.nn.silu(piece)
                z_ref[:, blk * BRANCH_WIDTH:(blk + 1) * BRANCH_WIDTH] = piece.astype(z_ref.dtype)


def _in_proj(x, g, w_all, mb, li):
    t = x.shape[0]
    tm = min(IN_PROJ_ROWS, t)
    tn = IN_PROJ_COLS
    assert (COL_A - COL_MERGE) * BRANCH_WIDTH % tn == 0
    return pl.pallas_call(
        _in_proj_kernel,
        out_shape=jax.ShapeDtypeStruct((t, Z_COLS), BF16),
        grid=(t // tm, Z_COLS // tn),
        in_specs=[
            pl.BlockSpec((tm, D_MODEL), lambda i, j: (i, 0)),
            pl.BlockSpec((1, D_MODEL), lambda i, j: (0, 0)),
            pl.BlockSpec((None, D_MODEL, tn), lambda i, j: (li, 0, j)),
            pl.BlockSpec((1, tn), lambda i, j: (0, j)),
        ],
        out_specs=pl.BlockSpec((tm, tn), lambda i, j: (i, j)),
        scratch_shapes=[pltpu.VMEM((tm, D_MODEL), BF16)],
        compiler_params=_cparams(("parallel", "arbitrary")),
        name="in_proj",
    )(x, g, w_all, mb)


def _attn_prep_kernel(q_ref, k_ref, v_ref, f_ref, fb_ref, gq_ref, gk_ref,
                      wq_ref, wk_ref, wv_ref, oq_ref, ok_ref, ov_ref,
                      qa_ref, ka_ref, va_ref, edge_ref, carry_ref):
    ts = q_ref.shape[1]

    @pl.when(pl.program_id(1) == 0)
    def _():
        carry_ref[...] = jnp.zeros_like(carry_ref)

    f = f_ref[0][:, :AUG].astype(F32) + fb_ref[...]
    log_f = (jnp.minimum(f, 0.0) - jnp.log1p(jnp.exp(-jnp.abs(f)))) * LOG2E
    tril = (_iota((ts, ts), 0) >= _iota((ts, ts), 1)).astype(BF16)
    part = _dot(tril, _pack3(log_f))
    total = (part + pltpu.roll(part, AUG - N_HEADS, axis=1)
             + pltpu.roll(part, AUG - 2 * N_HEADS, axis=1))
    cum = jnp.where(_iota((ts, AUG), 1) < N_HEADS, total, 0.0) + carry_ref[...]
    carry_ref[...] = cum[ts - 1:ts, :]
    edge_row = _iota((8, AUG), 0)
    edge_ref[0, 0] = jnp.where(edge_row == 0, cum[0:1, :],
                               jnp.where(edge_row == 1, cum[ts - 1:ts, :], 0.0))

    def normed(ref, gain_ref):
        x = ref[0].astype(F32)
        return (x * lax.rsqrt(_group_mean_sq(x) + EPS) * gain_ref[...]).astype(BF16)

    ccat = _pack3(cum)
    qa = _dot(jnp.concatenate([normed(q_ref, gq_ref), ccat], axis=1), wq_ref[...]) + oq_ref[...]
    ka = _dot(jnp.concatenate([normed(k_ref, gk_ref), ccat], axis=1), wk_ref[...]) + ok_ref[...]
    va = _dot(v_ref[0], wv_ref[...]) + ov_ref[...]
    for h in range(N_HEADS):
        lanes = slice(h * AUG, (h + 1) * AUG)
        qa_ref[0, h] = qa[:, lanes].astype(BF16)
        ka_ref[0, h] = ka[:, lanes].astype(BF16)
        va_ref[0, h] = va[:, lanes].astype(BF16)


def _attn_layout_constants():
    w, wide = BRANCH_WIDTH, N_HEADS * AUG
    place = np.zeros((w, wide), np.float32)
    sel_q = np.zeros((AUG, wide), np.float32)
    sel_k = np.zeros((AUG, wide), np.float32)
    ones_q = np.zeros((1, wide), np.float32)
    ones_k = np.zeros((1, wide), np.float32)
    ones_v = np.zeros((1, wide), np.float32)
    for h in range(N_HEADS):
        for d in range(HEAD_DIM):
            place[h * HEAD_DIM + d, h * AUG + d] = 1.0
        for i in range(3):
            sel_q[i * N_HEADS + h, h * AUG + LANE_C + i] = 1.0
            sel_k[i * N_HEADS + h, h * AUG + LANE_C + 3 + i] = -1.0
            ones_k[0, h * AUG + LANE_C + i] = 1.0
            ones_q[0, h * AUG + LANE_C + 3 + i] = 1.0
        ones_v[0, h * AUG + HEAD_DIM] = 1.0
    bf = lambda a: jnp.asarray(a, BF16)
    return (bf(np.concatenate([place, sel_q])), bf(np.concatenate([place, sel_k])), bf(place),
            jnp.asarray(ones_q), jnp.asarray(ones_k), jnp.asarray(ones_v))


def _attn_prep(z3, fbias, gq, gk):
    b, s, _ = z3.shape
    ts = _attn_block(s)
    col = lambda c: pl.BlockSpec((1, ts, BRANCH_WIDTH), lambda bi, si, c=c: (bi, si, c))
    full = lambda a: pl.BlockSpec(a.shape, lambda bi, si: (0, 0))
    aug = pl.BlockSpec((1, N_HEADS, ts, AUG), lambda bi, si: (bi, 0, si, 0))
    shape = jax.ShapeDtypeStruct((b, N_HEADS, s, AUG), BF16)
    edge_shape = jax.ShapeDtypeStruct((b, s // ts, 8, AUG), F32)
    consts = (fbias, gq, gk) + _attn_layout_constants()
    return pl.pallas_call(
        _attn_prep_kernel,
        out_shape=(shape, shape, shape, edge_shape),
        grid=(b, s // ts),
        in_specs=[col(COL_B), col(COL_B + 1), col(COL_B + 2), col(COL_F)] + [full(a) for a in consts],
        out_specs=(aug, aug, aug, pl.BlockSpec((1, 1, 8, AUG), lambda bi, si: (bi, si, 0, 0))),
        scratch_shapes=[pltpu.VMEM((1, AUG), F32)],
        compiler_params=_cparams(("parallel", "arbitrary")),
        name="attn_prep",
    )(z3, z3, z3, z3, *consts)


def _fox_attn_kernel(cfirst_ref, clast_ref, qa_ref, ka_ref, va_ref, o_ref, m_ref, acc_ref):
    bi = pl.program_id(0)
    qi = pl.program_id(1)
    nblk = pl.num_programs(1)
    tq = qa_ref.shape[2]
    tr = tq // ATTN_STREAMS

    def process(h, kj, nblocks, diagonal):
        start = pl.multiple_of(kj * tq, tq)
        widths = [(nblocks - 1) * tq + (r + 1) * tr if diagonal else nblocks * tq
                  for r in range(ATTN_STREAMS)]
        scores = [
            _dot_nt(qa_ref[0, h, r * tr:(r + 1) * tr, :],
                    ka_ref[0, h, pl.ds(start, widths[r]), :])
            for r in range(ATTN_STREAMS)]
        for r, s in enumerate(scores):
            wk = widths[r]
            if diagonal:
                edge = s[:, wk - tr:]
                edge = jnp.where(_iota((tr, tr), 1) <= _iota((tr, tr), 0), edge, MASK_VALUE)
                s = edge if wk == tr else jnp.concatenate([s[:, :wk - tr], edge], axis=1)
            m_prev = m_ref[h, r]
            m_new = jnp.maximum(m_prev, jnp.max(s, axis=-1, keepdims=True))
            alpha = jnp.exp2(m_prev - m_new)
            p = jnp.concatenate(
                [jnp.exp2(s[:, c * AUG:(c + 1) * AUG] - m_new) for c in range(wk // AUG)], axis=1)
            acc_ref[h, r] = acc_ref[h, r] * alpha + _dot(
                p.astype(BF16), va_ref[0, h, pl.ds(start, wk), :])
            m_ref[h, r] = m_new

    m_ref[...] = jnp.full_like(m_ref, MASK_VALUE)
    acc_ref[...] = jnp.zeros_like(acc_ref)

    last_off = jnp.maximum(qi - 1, 0)
    for h in range(N_HEADS):
        base = (bi * N_HEADS + h) * nblk
        c_i = cfirst_ref[base + qi]
        j_lo = lax.while_loop(
            lambda j: (j > 0) & (c_i - clast_ref[base + jnp.maximum(j - 1, 0)] >= 0.0),
            lambda j: j - 1, last_off)
        n_wide = (last_off - j_lo) // ATTN_WIDE

        def wide_body(t, carry, h=h, j_lo=j_lo):
            process(h, j_lo + t * ATTN_WIDE, ATTN_WIDE, False)
            return carry

        def single_body(j, carry, h=h):
            process(h, j, 1, False)
            return carry

        lax.fori_loop(0, n_wide, wide_body, 0)
        lax.fori_loop(j_lo + n_wide * ATTN_WIDE, last_off, single_body, 0)

    @pl.when(qi > 0)
    def _():
        for h in range(N_HEADS):
            process(h, qi - 1, 2, True)

    @pl.when(qi == 0)
    def _():
        for h in range(N_HEADS):
            process(h, qi, 1, True)

    outs = []
    for h in range(N_HEADS):
        acc = acc_ref[h].reshape(tq, AUG)
        outs.append(acc[:, :HEAD_DIM] / acc[:, HEAD_DIM:HEAD_DIM + 1])
    o_ref[0] = jnp.concatenate(outs, axis=-1)


def _fox_attn(qa, ka, va, cfirst, clast):
    b, _, s, _ = qa.shape
    tq = _attn_block(s)
    q_spec = pl.BlockSpec((1, N_HEADS, tq, AUG), lambda bi, i, cf, cl: (bi, 0, i, 0))
    kv_spec = pl.BlockSpec((1, N_HEADS, s, AUG), lambda bi, i, cf, cl: (bi, 0, 0, 0))
    return pl.pallas_call(
        _fox_attn_kernel,
        out_shape=jax.ShapeDtypeStruct((b, s, BRANCH_WIDTH), F32),
        grid_spec=pltpu.PrefetchScalarGridSpec(
            num_scalar_prefetch=2,
            grid=(b, s // tq),
            in_specs=[q_spec, kv_spec, kv_spec],
            out_specs=pl.BlockSpec((1, tq, BRANCH_WIDTH), lambda bi, i, cf, cl: (bi, i, 0)),
            scratch_shapes=[pltpu.VMEM((N_HEADS, ATTN_STREAMS, tq // ATTN_STREAMS, AUG), F32),
                            pltpu.VMEM((N_HEADS, ATTN_STREAMS, tq // ATTN_STREAMS, AUG), F32)],
        ),
        compiler_params=_cparams(("parallel", "arbitrary")),
        name="fox_attn",
    )(cfirst, clast, qa, ka, va)


def _hgrn_kernel(q_ref, f_ref, i_ref, lb_ref, gain_ref, o_ref,
                 state_ref, qs_ref, ks_ref, vs_ref, lg_ref, oc_ref):
    tc = q_ref.shape[1]
    c = GLA_CHUNK
    half = c // 2
    w = BRANCH_WIDTH

    @pl.when(pl.program_id(1) == 0)
    def _():
        state_ref[...] = jnp.zeros_like(state_ref)

    n = tc // c
    lb = lb_ref[...]
    bd_mask = _head_block_mask(w, w)

    q = q_ref[0].astype(F32)
    fl = f_ref[0].astype(F32)
    v = i_ref[0].astype(F32)
    t_small = jnp.exp(-jnp.abs(fl))
    s_big = 1.0 / (1.0 + t_small)
    s_small = t_small * s_big
    pos = fl >= 0.0
    log_g = jnp.log(lb + (1.0 - lb) * jnp.where(pos, s_big, s_small))
    kf = (1.0 - lb) * jnp.where(pos, s_small, s_big)

    r_i, c_i = _iota((tc, tc), 0), _iota((tc, tc), 1)
    tril = ((r_i >= c_i) & (r_i // c == c_i // c)).astype(BF16)
    l1, l2, l3 = _split3(log_g)
    bsum = _dot(tril, l1) + _dot(tril, l2) + _dot(tril, l3)

    b3 = bsum.reshape(n, c, w)
    b_first, b_q1 = b3[:, 0:1], b3[:, half // 2 - 1:half // 2]
    b_mid, b_mid1 = b3[:, half - 1:half], b3[:, half:half + 1]
    b_q3, b_last = b3[:, half + half // 2 - 1:half + half // 2], b3[:, c - 1:c]
    spread = jnp.maximum(jnp.maximum(b_first - b_q1, b_q1 - b_mid),
                         jnp.maximum(b_mid1 - b_q3, b_q3 - b_last))
    safe = jnp.max(spread) <= HGRN_SAFE_RANGE

    @pl.when(safe)
    def _():
        top = _iota((n, c, w), 1) < half
        ref_d = jnp.where(top, b_q1, b_q3)
        q3, k3 = q.reshape(n, c, w), kf.reshape(n, c, w)
        qd = q3 * jnp.exp(b3 - ref_d)
        kd = k3 * jnp.exp(ref_d - b3)
        qo = jnp.where(top, 0.0, q3 * jnp.exp(jnp.minimum(b3 - b_mid, 0.0)))
        ko = jnp.where(top, k3 * jnp.exp(jnp.minimum(b_mid - b3, 0.0)), 0.0)
        qcat = jnp.concatenate(
            [qo, jnp.where(top, qd, 0.0), jnp.where(top, 0.0, qd)], axis=2).astype(BF16)
        kparts = [x.astype(BF16) for x in (ko, jnp.where(top, kd, 0.0), jnp.where(top, 0.0, kd))]
        q_in = (q3 * jnp.exp(b3)).astype(BF16)
        k_end = (k3 * jnp.exp(b_last - b3)).astype(BF16)
        e_last = jnp.exp(b_last)
        v_b = v.astype(BF16).reshape(n, c, w)
        causal = (_iota((c, w), 1) % HEAD_DIM) <= _iota((c, w), 0)

        def block_diag(x):
            return jnp.where(bd_mask, jnp.concatenate([x] * N_HEADS, axis=0), 0.0)

        scores = []
        for ci in range(n):
            kcat = jnp.concatenate([block_diag(kp[ci]) for kp in kparts], axis=1)
            scores.append(jnp.where(causal, _dot_nt(qcat[ci], kcat), 0.0).astype(BF16))
        updates = [jnp.where(bd_mask, _dot_tn(v_b[ci], k_end[ci]), 0.0) for ci in range(n)]
        o_intra = [_dot(scores[ci], block_diag(v_b[ci])) for ci in range(n)]
        state_t = state_ref[...]
        outs = []
        for ci in range(n):
            outs.append(o_intra[ci] + _dot_nt(q_in[ci], state_t.astype(BF16)))
            state_t = state_t * e_last[ci] + updates[ci]
        state_ref[...] = state_t
        oc_ref[...] = jnp.concatenate(outs, axis=0)

    @pl.when(jnp.logical_not(safe))
    def _():
        qs_ref[...] = q
        ks_ref[...] = kf
        vs_ref[...] = v
        lg_ref[...] = log_g

        def body(t, carry):
            g_t = jnp.exp(lg_ref[pl.ds(t, 1), :])
            outer = _dot_tn(vs_ref[pl.ds(t, 1), :].astype(BF16),
                            ks_ref[pl.ds(t, 1), :].astype(BF16))
            st = state_ref[...] * g_t + jnp.where(bd_mask, outer, 0.0)
            state_ref[...] = st
            oc_ref[pl.ds(t, 1), :] = _dot_nt(qs_ref[pl.ds(t, 1), :].astype(BF16), st.astype(BF16))
            return carry

        lax.fori_loop(0, tc, body, 0)

    o = oc_ref[...]
    o_ref[0] = o * lax.rsqrt(_group_mean_sq(o) + EPS) * gain_ref[...]


def _hgrn(z3, lb, gain):
    b, s, _ = z3.shape
    tc = min(512, s)
    col = lambda c: pl.BlockSpec((1, tc, BRANCH_WIDTH), lambda bi, si, c=c: (bi, si, c))
    vec = pl.BlockSpec((1, BRANCH_WIDTH), lambda bi, si: (0, 0))
    tile = pltpu.VMEM((tc, BRANCH_WIDTH), F32)
    return pl.pallas_call(
        _hgrn_kernel,
        out_shape=jax.ShapeDtypeStruct((b, s, BRANCH_WIDTH), F32),
        grid=(b, s // tc),
        in_specs=[col(COL_C), col(COL_C + 1), col(COL_C + 2), vec, vec],
        out_specs=pl.BlockSpec((1, tc, BRANCH_WIDTH), lambda bi, si: (bi, si, 0)),
        scratch_shapes=[pltpu.VMEM((BRANCH_WIDTH, BRANCH_WIDTH), F32),
                        tile, tile, tile, tile, tile],
        compiler_params=_cparams(("parallel", "arbitrary")),
        name="hgrn2",
    )(z3, z3, z3, lb, gain)


HALO = 16


def _merge_kernel(zm_ref, za_ref, du_ref, dv_ref, dg_ref, bg_ref, cg_ref, hx_ref, hc_ref, ob_ref, oc_ref,
                  x_ref, p_ref, cw_ref, cb_ref, gv_ref, ws_ref, bs_ref, wup_ref,
                  wo_ref, gp_ref, wg_ref, wp_ref, out_ref, *, tiles_per_seq):
    tm = x_ref.shape[0]
    w = BRANCH_WIDTH

    za = za_ref[...].astype(F32)
    zc = za[:, 2 * w:3 * w] * za[:, 0:w]
    halo = hc_ref[...].astype(F32) * hx_ref[...].astype(F32)
    halo = jnp.where(pl.program_id(0) % tiles_per_seq == 0, 0.0, halo)
    ext = jnp.concatenate([halo, zc], axis=0)
    conv = zc * cw_ref[2:3, :]
    for tap in range(CONV_WIDTH - 1):
        shift = CONV_WIDTH - 1 - tap
        conv = conv + pltpu.roll(ext, shift, axis=0)[HALO:] * cw_ref[tap:tap + 1, :]
    y_a = za[:, w:2 * w] * (conv + cb_ref[...]) * za[:, 3 * w:4 * w]

    y_b = ob_ref[...] * bg_ref[...].astype(F32)
    y_c = oc_ref[...] * cg_ref[...].astype(F32)

    vd = dv_ref[...].astype(F32)
    vn = (vd * lax.rsqrt(_group_mean_sq(vd) + EPS) * gv_ref[...])
    lane_head = _iota((SPATIAL_CHUNK, w), 1) // HEAD_DIM
    mixed = []
    for ci in range(tm // SPATIAL_CHUNK):
        vc = vn[ci * SPATIAL_CHUNK:(ci + 1) * SPATIAL_CHUNK]
        stacked = jnp.concatenate(
            [jnp.where(lane_head == g, vc, 0.0) for g in range(N_HEADS)], axis=0).astype(BF16)
        mixed.append(_dot(ws_ref[...], stacked) + bs_ref[...])
    y_d = du_ref[...].astype(F32) * jnp.concatenate(mixed, axis=0) * dg_ref[...].astype(F32)

    merged = None
    for bi, y in enumerate((y_a, y_b, y_c, y_d)):
        gate = zm_ref[:, bi * D_MODEL:(bi + 1) * D_MODEL].astype(F32)
        term = gate * _dot(y.astype(BF16), wup_ref[bi])
        merged = term if merged is None else merged + term
    x1 = x_ref[...] + _dot(merged.astype(BF16), wo_ref[...])

    ms = jnp.mean(x1 * x1, axis=-1, keepdims=True)
    hp = (x1 * lax.rsqrt(ms + EPS) * gp_ref[...]).astype(BF16)
    ple = _dot(p_ref[...].astype(BF16), wp_ref[...])
    out_ref[...] = x1 + jax.nn.sigmoid(_dot(hp, wg_ref[...])) * ple


def _merge(z, o_b, o_c, x, p, seq_len, li, cw, cb, gv, ws, bs, wup, wo, gp, wg, wp):
    t = x.shape[0]
    tm = min(MERGE_ROWS, seq_len)
    w = BRANCH_WIDTH
    hstep = tm // HALO
    row = lambda width, c: pl.BlockSpec((tm, width), lambda i, c=c: (i, c))
    halo = lambda c: pl.BlockSpec((HALO, w), lambda i, c=c: (jnp.maximum(i * hstep - 1, 0), c))

    stacked = {id(a) for a in (ws, bs, wup, wo, wg, wp)}

    def full(a):
        if id(a) in stacked:
            return pl.BlockSpec((None,) + a.shape[1:], lambda i, n=a.ndim: (li,) + (0,) * (n - 1))
        return pl.BlockSpec(a.shape, lambda i, n=a.ndim: (0,) * n)

    consts = (cw, cb, gv, ws, bs, wup, wo, gp, wg, wp)
    return pl.pallas_call(
        functools.partial(_merge_kernel, tiles_per_seq=seq_len // tm),
        out_shape=jax.ShapeDtypeStruct((t, D_MODEL), F32),
        grid=(t // tm,),
        in_specs=[
            row(N_BRANCH * D_MODEL, COL_MERGE),
            row(4 * w, COL_A // 4),
            row(w, COL_D), row(w, COL_D + 1), row(w, COL_D + 2),
            row(w, COL_B + 3), row(w, COL_C + 3),
            halo(COL_A), halo(COL_A + 2),
            row(w, 0), row(w, 0),
            row(D_MODEL, 0),
            pl.BlockSpec((None, tm, PLE_DIM), lambda i: (li, i, 0)),
        ] + [full(a) for a in consts],
        out_specs=pl.BlockSpec((tm, D_MODEL), lambda i: (i, 0)),
        compiler_params=_cparams(("parallel",)),
        name="merge",
    )(z, z, z, z, z, z, z, z, z, o_b, o_c, x, p, *consts)


def _reorder_w_in(w_in):
    w = BRANCH_WIDTH
    n_pre = 8 * w
    n_f = N_HEADS
    n_cd = 7 * w
    pre = w_in[:, :, :n_pre]
    fcols = w_in[:, :, n_pre:n_pre + n_f]
    cd = w_in[:, :, n_pre + n_f:n_pre + n_f + n_cd]
    merge = w_in[:, :, n_pre + n_f + n_cd:]
    pad = jnp.zeros(w_in.shape[:2] + (w - n_f,), w_in.dtype)
    return jnp.concatenate([merge, pre, cd, fcols, pad], axis=-1).astype(BF16)


def kernel(x, p, norm_mix, w_in, conv_w, conv_b, fgate_bias, q_norm, k_norm, lb_logits,
           hgrn_norm, sgu_norm, spatial_w, spatial_b, w_up, merge_b, w_o, norm_ple,
           w_ple_gate, w_ple_proj):
    bn, s, _ = x.shape
    depth = w_in.shape[0]
    t = bn * s
    assert w_in.shape[-1] == 15 * BRANCH_WIDTH + N_HEADS + N_BRANCH * D_MODEL
    assert s % SPATIAL_CHUNK == 0 and s % GLA_CHUNK == 0

    lb_p = jax.nn.softmax(lb_logits.astype(F32), axis=0)
    lower_bounds = jnp.clip(jnp.cumsum(lb_p, axis=0) - lb_p[0], 0.0, 1.0)
    w_z = _reorder_w_in(w_in)
    fbias = jnp.pad(fgate_bias.astype(F32), ((0, 0), (0, AUG - N_HEADS)))
    gq = jnp.tile(q_norm.astype(F32) * (HEAD_DIM ** -0.5 * LOG2E), (1, N_HEADS))
    gk = jnp.tile(k_norm.astype(F32), (1, N_HEADS))
    qk_bound = (HEAD_DIM * BF16_NORM_MARGIN) * jnp.max(jnp.abs(gq), axis=1) * jnp.max(jnp.abs(gk), axis=1)
    skip_slack = 2.0 * qk_bound + ATTN_CUTOFF_LOG2
    causal = jnp.tril(jnp.ones((SPATIAL_CHUNK, SPATIAL_CHUNK), F32))
    ws = (spatial_w.astype(F32) * causal).transpose(0, 2, 1, 3).reshape(
        depth, SPATIAL_CHUNK, N_HEADS * SPATIAL_CHUNK).astype(BF16)
    bs = jnp.repeat(spatial_b.astype(F32).transpose(0, 2, 1), HEAD_DIM, axis=-1)
    mb_cols = jnp.pad(merge_b.astype(F32).reshape(depth, 1, N_BRANCH * D_MODEL),
                      ((0, 0), (0, 0), (COL_MERGE * BRANCH_WIDTH,
                                        Z_COLS - N_BRANCH * D_MODEL - COL_MERGE * BRANCH_WIDTH)))
    w_up_b = w_up.astype(BF16)
    w_o_b = w_o.astype(BF16)
    w_g_b = w_ple_gate.astype(BF16)
    w_p_b = w_ple_proj.astype(BF16)

    xf = x.reshape(t, D_MODEL)
    p_rows = p.reshape(depth, t, PLE_DIM)
    for li in range(depth):
        z = _in_proj(xf, norm_mix[li][None], w_z, mb_cols[li], li)
        z3 = z.reshape(bn, s, Z_COLS)
        qa, ka, va, edge = _attn_prep(z3, fbias[li][None], gq[li][None], gk[li][None])
        cfirst = edge[:, :, 0, :N_HEADS].transpose(0, 2, 1).reshape(-1) + skip_slack[li]
        clast = edge[:, :, 1, :N_HEADS].transpose(0, 2, 1).reshape(-1)
        o_b = _fox_attn(qa, ka, va, cfirst, clast).reshape(t, BRANCH_WIDTH)
        o_c = _hgrn(z3, lower_bounds[li][None], hgrn_norm[li][None]).reshape(t, BRANCH_WIDTH)
        xf = _merge(z, o_b, o_c, xf, p_rows, s, li,
                    conv_w[li], conv_b[li][None], sgu_norm[li][None], ws, bs,
                    w_up_b, w_o_b, norm_ple[li][None], w_g_b, w_p_b)
    return xf.reshape(bn, s, D_MODEL).astype(x.dtype)
```

```python
import functools
import math

import jax
import jax.numpy as jnp
import numpy as np
from jax import lax
from jax.experimental import pallas as pl
from jax.experimental.pallas import tpu as pltpu

F32 = jnp.float32
BF16 = jnp.bfloat16

D_MODEL = 1024
PLE_DIM = 256
N_BRANCH = 4
BRANCH_WIDTH = 256
HEAD_DIM = 64
N_HEADS = BRANCH_WIDTH // HEAD_DIM
CONV_WIDTH = 3
GLA_CHUNK = 64
SPATIAL_CHUNK = 128
EPS = 1e-6
MASK_VALUE = -1e30
LOG2E = math.log2(math.e)

Z_COLS = 8192
COL_MERGE = 0
COL_A = 16
COL_B = 20
COL_C = 24
COL_D = 28
COL_F = 31
SILU_COLS = (COL_A + 3, COL_B + 3, COL_C, COL_C + 3, COL_D + 2)

AUG = 128
LANE_C = HEAD_DIM
IN_PROJ_ROWS = 1024
IN_PROJ_COLS = 2048
MERGE_ROWS = 512
ATTN_BLOCK = 512
ATTN_STREAMS = 4
ATTN_WIDE = 2
ATTN_CUTOFF_LOG2 = 152.0
BF16_NORM_MARGIN = 1.02

VMEM_LIMIT_BYTES = 56 * 1024 * 1024
HGRN_SAFE_RANGE = 60.0


def _attn_block(seq_len):
    return min(ATTN_BLOCK, seq_len)


def _cparams(semantics):
    return pltpu.CompilerParams(dimension_semantics=semantics, vmem_limit_bytes=VMEM_LIMIT_BYTES)


def _split3(x):
    a = x.astype(BF16)
    r = x - a.astype(F32)
    b = r.astype(BF16)
    c = (r - b.astype(F32)).astype(BF16)
    return a, b, c


def _pack3(x):
    a, b, c = (piece.astype(F32) for piece in _split3(x))
    lane = _iota(x.shape, 1)
    packed = jnp.where(
        lane < N_HEADS, a,
        jnp.where(lane < 2 * N_HEADS, pltpu.roll(b, N_HEADS, axis=1),
                  jnp.where(lane < 3 * N_HEADS, pltpu.roll(c, 2 * N_HEADS, axis=1), 0.0)))
    return packed.astype(BF16)


def _dot(a, b):
    return jnp.dot(a, b, preferred_element_type=F32)


def _dot_nt(a, b):
    return lax.dot_general(a, b, (((1,), (1,)), ((), ())), preferred_element_type=F32)


def _dot_tn(a, b):
    return lax.dot_general(a, b, (((0,), (0,)), ((), ())), preferred_element_type=F32)


def _iota(shape, dim):
    return lax.broadcasted_iota(jnp.int32, shape, dim)


def _head_block_mask(rows, cols):
    return (_iota((rows, cols), 0) // HEAD_DIM) == (_iota((rows, cols), 1) // HEAD_DIM)


def _group_mean_sq(x):
    w = x.shape[-1]
    avg = jnp.where(_head_block_mask(w, w), 1.0 / HEAD_DIM, 0.0).astype(BF16)
    sq = x * x
    hi = sq.astype(BF16)
    lo = (sq - hi.astype(F32)).astype(BF16)
    return _dot(hi, avg) + _dot(lo, avg)


def _in_proj_kernel(x_ref, g_ref, w_ref, mb_ref, z_ref, h_ref):
    j = pl.program_id(1)
    tn = z_ref.shape[1]
    blocks_per_step = tn // BRANCH_WIDTH

    @pl.when(j == 0)
    def _():
        x = x_ref[...]
        ms = jnp.mean(x * x, axis=-1, keepdims=True)
        h_ref[...] = (x * lax.rsqrt(ms + EPS) * g_ref[...]).astype(BF16)

    n_merge_steps = (COL_A - COL_MERGE) // blocks_per_step

    @pl.when(j < n_merge_steps)
    def _():
        res = _dot(h_ref[...], w_ref[...])
        half_logit = (0.5 * (res + mb_ref[...])).astype(z_ref.dtype)
        z_ref[...] = 0.5 * jnp.tanh(half_logit) + 0.5

    for step in range(n_merge_steps, Z_COLS // tn):
        @pl.when(j == step)
        def _(step=step):
            res = _dot(h_ref[...], w_ref[...])
            for blk in range(blocks_per_step):
                piece = res[:, blk * BRANCH_WIDTH:(blk + 1) * BRANCH_WIDTH]
                if step * blocks_per_step + blk in SILU_COLS:
                    piece = jax.nn.silu(piece)
                z_ref[:, blk * BRANCH_WIDTH:(blk + 1) * BRANCH_WIDTH] = piece.astype(z_ref.dtype)


def _in_proj(x, g, w_all, mb, li):
    t = x.shape[0]
    tm = min(IN_PROJ_ROWS, t)
    tn = IN_PROJ_COLS
    assert (COL_A - COL_MERGE) * BRANCH_WIDTH % tn == 0
    return pl.pallas_call(
        _in_proj_kernel,
        out_shape=jax.ShapeDtypeStruct((t, Z_COLS), BF16),
        grid=(t // tm, Z_COLS // tn),
        in_specs=[
            pl.BlockSpec((tm, D_MODEL), lambda i, j: (i, 0)),
            pl.BlockSpec((1, D_MODEL), lambda i, j: (0, 0)),
            pl.BlockSpec((None, D_MODEL, tn), lambda i, j: (li, 0, j)),
            pl.BlockSpec((1, tn), lambda i, j: (0, j)),
        ],
        out_specs=pl.BlockSpec((tm, tn), lambda i, j: (i, j)),
        scratch_shapes=[pltpu.VMEM((tm, D_MODEL), BF16)],
        compiler_params=_cparams(("parallel", "arbitrary")),
        name="in_proj",
    )(x, g, w_all, mb)


def _attn_prep_body(q_ref, k_ref, v_ref, f_ref, fb_ref, gq_ref, gk_ref,
                    wq_ref, wk_ref, wv_ref, oq_ref, ok_ref, ov_ref,
                    qa_ref, ka_ref, va_ref, edge_ref, carry_ref):
    ts = q_ref.shape[1]

    f = f_ref[0][:, :AUG].astype(F32) + fb_ref[...]
    log_f = (jnp.minimum(f, 0.0) - jnp.log1p(jnp.exp(-jnp.abs(f)))) * LOG2E
    tril = (_iota((ts, ts), 0) >= _iota((ts, ts), 1)).astype(BF16)
    part = _dot(tril, _pack3(log_f))
    total = (part + pltpu.roll(part, AUG - N_HEADS, axis=1)
             + pltpu.roll(part, AUG - 2 * N_HEADS, axis=1))
    cum = jnp.where(_iota((ts, AUG), 1) < N_HEADS, total, 0.0) + carry_ref[...]
    carry_ref[...] = cum[ts - 1:ts, :]
    edge_row = _iota((8, AUG), 0)
    edge_ref[0, 0] = jnp.where(edge_row == 0, cum[0:1, :],
                               jnp.where(edge_row == 1, cum[ts - 1:ts, :], 0.0))

    def normed(ref, gain_ref):
        x = ref[0].astype(F32)
        return (x * lax.rsqrt(_group_mean_sq(x) + EPS) * gain_ref[...]).astype(BF16)

    ccat = _pack3(cum)
    qa = _dot(jnp.concatenate([normed(q_ref, gq_ref), ccat], axis=1), wq_ref[...]) + oq_ref[...]
    ka = _dot(jnp.concatenate([normed(k_ref, gk_ref), ccat], axis=1), wk_ref[...]) + ok_ref[...]
    va = _dot(v_ref[0], wv_ref[...]) + ov_ref[...]
    for h in range(N_HEADS):
        lanes = slice(h * AUG, (h + 1) * AUG)
        qa_ref[0, h] = qa[:, lanes].astype(BF16)
        ka_ref[0, h] = ka[:, lanes].astype(BF16)
        va_ref[0, h] = va[:, lanes].astype(BF16)


def _attn_layout_constants():
    w, wide = BRANCH_WIDTH, N_HEADS * AUG
    place = np.zeros((w, wide), np.float32)
    sel_q = np.zeros((AUG, wide), np.float32)
    sel_k = np.zeros((AUG, wide), np.float32)
    ones_q = np.zeros((1, wide), np.float32)
    ones_k = np.zeros((1, wide), np.float32)
    ones_v = np.zeros((1, wide), np.float32)
    for h in range(N_HEADS):
        for d in range(HEAD_DIM):
            place[h * HEAD_DIM + d, h * AUG + d] = 1.0
        for i in range(3):
            sel_q[i * N_HEADS + h, h * AUG + LANE_C + i] = 1.0
            sel_k[i * N_HEADS + h, h * AUG + LANE_C + 3 + i] = -1.0
            ones_k[0, h * AUG + LANE_C + i] = 1.0
            ones_q[0, h * AUG + LANE_C + 3 + i] = 1.0
        ones_v[0, h * AUG + HEAD_DIM] = 1.0
    bf = lambda a: jnp.asarray(a, BF16)
    return (bf(np.concatenate([place, sel_q])), bf(np.concatenate([place, sel_k])), bf(place),
            jnp.asarray(ones_q), jnp.asarray(ones_k), jnp.asarray(ones_v))


def _fox_attn_kernel(cfirst_ref, clast_ref, qa_ref, ka_ref, va_ref, o_ref, m_ref, acc_ref):
    bi = pl.program_id(0)
    qi = pl.program_id(1)
    nblk = pl.num_programs(1)
    tq = qa_ref.shape[2]
    tr = tq // ATTN_STREAMS

    def process(h, kj, nblocks, diagonal):
        start = pl.multiple_of(kj * tq, tq)
        widths = [(nblocks - 1) * tq + (r + 1) * tr if diagonal else nblocks * tq
                  for r in range(ATTN_STREAMS)]
        scores = [
            _dot_nt(qa_ref[0, h, r * tr:(r + 1) * tr, :],
                    ka_ref[0, h, pl.ds(start, widths[r]), :])
            for r in range(ATTN_STREAMS)]
        for r, s in enumerate(scores):
            wk = widths[r]
            if diagonal:
                edge = s[:, wk - tr:]
                edge = jnp.where(_iota((tr, tr), 1) <= _iota((tr, tr), 0), edge, MASK_VALUE)
                s = edge if wk == tr else jnp.concatenate([s[:, :wk - tr], edge], axis=1)
            m_prev = m_ref[h, r]
            m_new = jnp.maximum(m_prev, jnp.max(s, axis=-1, keepdims=True))
            alpha = jnp.exp2(m_prev - m_new)
            p = jnp.concatenate(
                [jnp.exp2(s[:, c * AUG:(c + 1) * AUG] - m_new) for c in range(wk // AUG)], axis=1)
            acc_ref[h, r] = acc_ref[h, r] * alpha + _dot(
                p.astype(BF16), va_ref[0, h, pl.ds(start, wk), :])
            m_ref[h, r] = m_new

    m_ref[...] = jnp.full_like(m_ref, MASK_VALUE)
    acc_ref[...] = jnp.zeros_like(acc_ref)

    last_off = jnp.maximum(qi - 1, 0)
    for h in range(N_HEADS):
        base = (bi * N_HEADS + h) * nblk
        c_i = cfirst_ref[base + qi]
        j_lo = lax.while_loop(
            lambda j: (j > 0) & (c_i - clast_ref[base + jnp.maximum(j - 1, 0)] >= 0.0),
            lambda j: j - 1, last_off)
        n_wide = (last_off - j_lo) // ATTN_WIDE

        def wide_body(t, carry, h=h, j_lo=j_lo):
            process(h, j_lo + t * ATTN_WIDE, ATTN_WIDE, False)
            return carry

        def single_body(j, carry, h=h):
            process(h, j, 1, False)
            return carry

        lax.fori_loop(0, n_wide, wide_body, 0)
        lax.fori_loop(j_lo + n_wide * ATTN_WIDE, last_off, single_body, 0)

    @pl.when(qi > 0)
    def _():
        for h in range(N_HEADS):
            process(h, qi - 1, 2, True)

    @pl.when(qi == 0)
    def _():
        for h in range(N_HEADS):
            process(h, qi, 1, True)

    outs = []
    for h in range(N_HEADS):
        acc = acc_ref[h].reshape(tq, AUG)
        outs.append(acc[:, :HEAD_DIM] / acc[:, HEAD_DIM:HEAD_DIM + 1])
    o_ref[0] = jnp.concatenate(outs, axis=-1)


def _fox_attn(qa, ka, va, cfirst, clast):
    b, _, s, _ = qa.shape
    tq = _attn_block(s)
    q_spec = pl.BlockSpec((1, N_HEADS, tq, AUG), lambda bi, i, cf, cl: (bi, 0, i, 0))
    kv_spec = pl.BlockSpec((1, N_HEADS, s, AUG), lambda bi, i, cf, cl: (bi, 0, 0, 0))
    return pl.pallas_call(
        _fox_attn_kernel,
        out_shape=jax.ShapeDtypeStruct((b, s, BRANCH_WIDTH), F32),
        grid_spec=pltpu.PrefetchScalarGridSpec(
            num_scalar_prefetch=2,
            grid=(b, s // tq),
            in_specs=[q_spec, kv_spec, kv_spec],
            out_specs=pl.BlockSpec((1, tq, BRANCH_WIDTH), lambda bi, i, cf, cl: (bi, i, 0)),
            scratch_shapes=[pltpu.VMEM((N_HEADS, ATTN_STREAMS, tq // ATTN_STREAMS, AUG), F32),
                            pltpu.VMEM((N_HEADS, ATTN_STREAMS, tq // ATTN_STREAMS, AUG), F32)],
        ),
        compiler_params=_cparams(("parallel", "arbitrary")),
        name="fox_attn",
    )(cfirst, clast, qa, ka, va)


def _hgrn_body(q_ref, f_ref, i_ref, lb_ref, gain_ref, o_ref,
               state_ref, qs_ref, ks_ref, vs_ref, lg_ref, oc_ref):
    tc = q_ref.shape[1]
    c = GLA_CHUNK
    half = c // 2
    w = BRANCH_WIDTH
    n = tc // c
    lb = lb_ref[...]
    bd_mask = _head_block_mask(w, w)

    q = q_ref[0].astype(F32)
    fl = f_ref[0].astype(F32)
    v = i_ref[0].astype(F32)
    t_small = jnp.exp(-jnp.abs(fl))
    s_big = 1.0 / (1.0 + t_small)
    s_small = t_small * s_big
    pos = fl >= 0.0
    log_g = jnp.log(lb + (1.0 - lb) * jnp.where(pos, s_big, s_small))
    kf = (1.0 - lb) * jnp.where(pos, s_small, s_big)

    r_i, c_i = _iota((tc, tc), 0), _iota((tc, tc), 1)
    tril = ((r_i >= c_i) & (r_i // c == c_i // c)).astype(BF16)
    l1, l2, l3 = _split3(log_g)
    bsum = _dot(tril, l1) + _dot(tril, l2) + _dot(tril, l3)

    b3 = bsum.reshape(n, c, w)
    b_first, b_q1 = b3[:, 0:1], b3[:, half // 2 - 1:half // 2]
    b_mid, b_mid1 = b3[:, half - 1:half], b3[:, half:half + 1]
    b_q3, b_last = b3[:, half + half // 2 - 1:half + half // 2], b3[:, c - 1:c]
    spread = jnp.maximum(jnp.maximum(b_first - b_q1, b_q1 - b_mid),
                         jnp.maximum(b_mid1 - b_q3, b_q3 - b_last))
    safe = jnp.max(spread) <= HGRN_SAFE_RANGE

    @pl.when(safe)
    def _():
        top = _iota((n, c, w), 1) < half
        ref_d = jnp.where(top, b_q1, b_q3)
        q3, k3 = q.reshape(n, c, w), kf.reshape(n, c, w)
        qd = q3 * jnp.exp(b3 - ref_d)
        kd = k3 * jnp.exp(ref_d - b3)
        qo = jnp.where(top, 0.0, q3 * jnp.exp(jnp.minimum(b3 - b_mid, 0.0)))
        ko = jnp.where(top, k3 * jnp.exp(jnp.minimum(b_mid - b3, 0.0)), 0.0)
        qcat = jnp.concatenate(
            [qo, jnp.where(top, qd, 0.0), jnp.where(top, 0.0, qd)], axis=2).astype(BF16)
        kparts = [x.astype(BF16) for x in (ko, jnp.where(top, kd, 0.0), jnp.where(top, 0.0, kd))]
        q_in = (q3 * jnp.exp(b3)).astype(BF16)
        k_end = (k3 * jnp.exp(b_last - b3)).astype(BF16)
        e_last = jnp.exp(b_last)
        v_b = v.astype(BF16).reshape(n, c, w)
        causal = (_iota((c, w), 1) % HEAD_DIM) <= _iota((c, w), 0)

        def block_diag(x):
            return jnp.where(bd_mask, jnp.concatenate([x] * N_HEADS, axis=0), 0.0)

        scores = []
        for ci in range(n):
            kcat = jnp.concatenate([block_diag(kp[ci]) for kp in kparts], axis=1)
            scores.append(jnp.where(causal, _dot_nt(qcat[ci], kcat), 0.0).astype(BF16))
        updates = [jnp.where(bd_mask, _dot_tn(v_b[ci], k_end[ci]), 0.0) for ci in range(n)]
        o_intra = [_dot(scores[ci], block_diag(v_b[ci])) for ci in range(n)]
        state_t = state_ref[...]
        outs = []
        for ci in range(n):
            outs.append(o_intra[ci] + _dot_nt(q_in[ci], state_t.astype(BF16)))
            state_t = state_t * e_last[ci] + updates[ci]
        state_ref[...] = state_t
        oc_ref[...] = jnp.concatenate(outs, axis=0)

    @pl.when(jnp.logical_not(safe))
    def _():
        qs_ref[...] = q
        ks_ref[...] = kf
        vs_ref[...] = v
        lg_ref[...] = log_g

        def body(t, carry):
            g_t = jnp.exp(lg_ref[pl.ds(t, 1), :])
            outer = _dot_tn(vs_ref[pl.ds(t, 1), :].astype(BF16),
                            ks_ref[pl.ds(t, 1), :].astype(BF16))
            st = state_ref[...] * g_t + jnp.where(bd_mask, outer, 0.0)
            state_ref[...] = st
            oc_ref[pl.ds(t, 1), :] = _dot_nt(qs_ref[pl.ds(t, 1), :].astype(BF16), st.astype(BF16))
            return carry

        lax.fori_loop(0, tc, body, 0)

    o = oc_ref[...]
    o_ref[0] = o * lax.rsqrt(_group_mean_sq(o) + EPS) * gain_ref[...]


N_PREP_IN = 13
N_HGRN_IN = 5


def _seq_mixers_kernel(*refs):
    prep_in = refs[:N_PREP_IN]
    hgrn_in = refs[N_PREP_IN:N_PREP_IN + N_HGRN_IN]
    qa_ref, ka_ref, va_ref, edge_ref, o_ref = refs[N_PREP_IN + N_HGRN_IN:N_PREP_IN + N_HGRN_IN + 5]
    carry_ref, state_ref, qs_ref, ks_ref, vs_ref, lg_ref, oc_ref = refs[N_PREP_IN + N_HGRN_IN + 5:]

    @pl.when(pl.program_id(1) == 0)
    def _():
        carry_ref[...] = jnp.zeros_like(carry_ref)
        state_ref[...] = jnp.zeros_like(state_ref)

    _attn_prep_body(*prep_in, qa_ref, ka_ref, va_ref, edge_ref, carry_ref)
    _hgrn_body(*hgrn_in, o_ref, state_ref, qs_ref, ks_ref, vs_ref, lg_ref, oc_ref)


def _seq_mixers(z3, fbias, gq, gk, lb, gain):
    b, s, _ = z3.shape
    ts = _attn_block(s)
    col = lambda c: pl.BlockSpec((1, ts, BRANCH_WIDTH), lambda bi, si, c=c: (bi, si, c))
    full = lambda a: pl.BlockSpec(a.shape, lambda bi, si: (0, 0))
    aug = pl.BlockSpec((1, N_HEADS, ts, AUG), lambda bi, si: (bi, 0, si, 0))
    aug_shape = jax.ShapeDtypeStruct((b, N_HEADS, s, AUG), BF16)
    edge_shape = jax.ShapeDtypeStruct((b, s // ts, 8, AUG), F32)
    prep_consts = (fbias, gq, gk) + _attn_layout_constants()
    prep_specs = [col(COL_B), col(COL_B + 1), col(COL_B + 2), col(COL_F)] + [full(a) for a in prep_consts]
    hgrn_specs = [col(COL_C), col(COL_C + 1), col(COL_C + 2), full(lb), full(gain)]
    assert len(prep_specs) == N_PREP_IN and len(hgrn_specs) == N_HGRN_IN
    tile = pltpu.VMEM((ts, BRANCH_WIDTH), F32)
    return pl.pallas_call(
        _seq_mixers_kernel,
        out_shape=(aug_shape, aug_shape, aug_shape, edge_shape,
                   jax.ShapeDtypeStruct((b, s, BRANCH_WIDTH), F32)),
        grid=(b, s // ts),
        in_specs=prep_specs + hgrn_specs,
        out_specs=(aug, aug, aug, pl.BlockSpec((1, 1, 8, AUG), lambda bi, si: (bi, si, 0, 0)),
                   pl.BlockSpec((1, ts, BRANCH_WIDTH), lambda bi, si: (bi, si, 0))),
        scratch_shapes=[pltpu.VMEM((1, AUG), F32),
                        pltpu.VMEM((BRANCH_WIDTH, BRANCH_WIDTH), F32),
                        tile, tile, tile, tile, tile],
        compiler_params=_cparams(("parallel", "arbitrary")),
        name="seq_mixers",
    )(z3, z3, z3, z3, *prep_consts, z3, z3, z3, lb, gain)


HALO = 16


def _merge_kernel(zm_ref, za_ref, du_ref, dv_ref, dg_ref, bg_ref, cg_ref, hx_ref, hc_ref, ob_ref, oc_ref,
                  x_ref, p_ref, cw_ref, cb_ref, gv_ref, ws_ref, bs_ref, wup_ref,
                  wo_ref, gp_ref, wg_ref, wp_ref, out_ref, *, tiles_per_seq):
    tm = x_ref.shape[0]
    w = BRANCH_WIDTH

    za = za_ref[...].astype(F32)
    zc = za[:, 2 * w:3 * w] * za[:, 0:w]
    halo = hc_ref[...].astype(F32) * hx_ref[...].astype(F32)
    halo = jnp.where(pl.program_id(0) % tiles_per_seq == 0, 0.0, halo)
    ext = jnp.concatenate([halo, zc], axis=0)
    conv = zc * cw_ref[2:3, :]
    for tap in range(CONV_WIDTH - 1):
        shift = CONV_WIDTH - 1 - tap
        conv = conv + pltpu.roll(ext, shift, axis=0)[HALO:] * cw_ref[tap:tap + 1, :]
    y_a = za[:, w:2 * w] * (conv + cb_ref[...]) * za[:, 3 * w:4 * w]

    y_b = ob_ref[...] * bg_ref[...].astype(F32)
    y_c = oc_ref[...] * cg_ref[...].astype(F32)

    vd = dv_ref[...].astype(F32)
    vn = (vd * lax.rsqrt(_group_mean_sq(vd) + EPS) * gv_ref[...])
    lane_head = _iota((SPATIAL_CHUNK, w), 1) // HEAD_DIM
    mixed = []
    for ci in range(tm // SPATIAL_CHUNK):
        vc = vn[ci * SPATIAL_CHUNK:(ci + 1) * SPATIAL_CHUNK]
        stacked = jnp.concatenate(
            [jnp.where(lane_head == g, vc, 0.0) for g in range(N_HEADS)], axis=0).astype(BF16)
        mixed.append(_dot(ws_ref[...], stacked) + bs_ref[...])
    y_d = du_ref[...].astype(F32) * jnp.concatenate(mixed, axis=0) * dg_ref[...].astype(F32)

    merged = None
    for bi, y in enumerate((y_a, y_b, y_c, y_d)):
        gate = zm_ref[:, bi * D_MODEL:(bi + 1) * D_MODEL].astype(F32)
        term = gate * _dot(y.astype(BF16), wup_ref[bi])
        merged = term if merged is None else merged + term
    x1 = x_ref[...] + _dot(merged.astype(BF16), wo_ref[...])

    ms = jnp.mean(x1 * x1, axis=-1, keepdims=True)
    hp = (x1 * lax.rsqrt(ms + EPS) * gp_ref[...]).astype(BF16)
    ple = _dot(p_ref[...].astype(BF16), wp_ref[...])
    out_ref[...] = x1 + jax.nn.sigmoid(_dot(hp, wg_ref[...])) * ple


def _merge(z, o_b, o_c, x, p, seq_len, li, cw, cb, gv, ws, bs, wup, wo, gp, wg, wp):
    t = x.shape[0]
    tm = min(MERGE_ROWS, seq_len)
    w = BRANCH_WIDTH
    hstep = tm // HALO
    row = lambda width, c: pl.BlockSpec((tm, width), lambda i, c=c: (i, c))
    halo = lambda c: pl.BlockSpec((HALO, w), lambda i, c=c: (jnp.maximum(i * hstep - 1, 0), c))

    stacked = {id(a) for a in (ws, bs, wup, wo, wg, wp)}

    def full(a):
        if id(a) in stacked:
            return pl.BlockSpec((None,) + a.shape[1:], lambda i, n=a.ndim: (li,) + (0,) * (n - 1))
        return pl.BlockSpec(a.shape, lambda i, n=a.ndim: (0,) * n)

    consts = (cw, cb, gv, ws, bs, wup, wo, gp, wg, wp)
    return pl.pallas_call(
        functools.partial(_merge_kernel, tiles_per_seq=seq_len // tm),
        out_shape=jax.ShapeDtypeStruct((t, D_MODEL), F32),
        grid=(t // tm,),
        in_specs=[
            row(N_BRANCH * D_MODEL, COL_MERGE),
            row(4 * w, COL_A // 4),
            row(w, COL_D), row(w, COL_D + 1), row(w, COL_D + 2),
            row(w, COL_B + 3), row(w, COL_C + 3),
            halo(COL_A), halo(COL_A + 2),
            row(w, 0), row(w, 0),
            row(D_MODEL, 0),
            pl.BlockSpec((None, tm, PLE_DIM), lambda i: (li, i, 0)),
        ] + [full(a) for a in consts],
        out_specs=pl.BlockSpec((tm, D_MODEL), lambda i: (i, 0)),
        compiler_params=_cparams(("parallel",)),
        name="merge",
    )(z, z, z, z, z, z, z, z, z, o_b, o_c, x, p, *consts)


def _reorder_w_in(w_in):
    w = BRANCH_WIDTH
    n_pre = 8 * w
    n_f = N_HEADS
    n_cd = 7 * w
    pre = w_in[:, :, :n_pre]
    fcols = w_in[:, :, n_pre:n_pre + n_f]
    cd = w_in[:, :, n_pre + n_f:n_pre + n_f + n_cd]
    merge = w_in[:, :, n_pre + n_f + n_cd:]
    pad = jnp.zeros(w_in.shape[:2] + (w - n_f,), w_in.dtype)
    return jnp.concatenate([merge, pre, cd, fcols, pad], axis=-1).astype(BF16)


def kernel(x, p, norm_mix, w_in, conv_w, conv_b, fgate_bias, q_norm, k_norm, lb_logits,
           hgrn_norm, sgu_norm, spatial_w, spatial_b, w_up, merge_b, w_o, norm_ple,
           w_ple_gate, w_ple_proj):
    bn, s, _ = x.shape
    depth = w_in.shape[0]
    t = bn * s
    assert w_in.shape[-1] == 15 * BRANCH_WIDTH + N_HEADS + N_BRANCH * D_MODEL
    assert s % SPATIAL_CHUNK == 0 and s % GLA_CHUNK == 0

    lb_p = jax.nn.softmax(lb_logits.astype(F32), axis=0)
    lower_bounds = jnp.clip(jnp.cumsum(lb_p, axis=0) - lb_p[0], 0.0, 1.0)
    w_z = _reorder_w_in(w_in)
    fbias = jnp.pad(fgate_bias.astype(F32), ((0, 0), (0, AUG - N_HEADS)))
    gq = jnp.tile(q_norm.astype(F32) * (HEAD_DIM ** -0.5 * LOG2E), (1, N_HEADS))
    gk = jnp.tile(k_norm.astype(F32), (1, N_HEADS))
    qk_bound = (HEAD_DIM * BF16_NORM_MARGIN) * jnp.max(jnp.abs(gq), axis=1) * jnp.max(jnp.abs(gk), axis=1)
    skip_slack = 2.0 * qk_bound + ATTN_CUTOFF_LOG2
    causal = jnp.tril(jnp.ones((SPATIAL_CHUNK, SPATIAL_CHUNK), F32))
    ws = (spatial_w.astype(F32) * causal).transpose(0, 2, 1, 3).reshape(
        depth, SPATIAL_CHUNK, N_HEADS * SPATIAL_CHUNK).astype(BF16)
    bs = jnp.repeat(spatial_b.astype(F32).transpose(0, 2, 1), HEAD_DIM, axis=-1)
    mb_cols = jnp.pad(merge_b.astype(F32).reshape(depth, 1, N_BRANCH * D_MODEL),
                      ((0, 0), (0, 0), (COL_MERGE * BRANCH_WIDTH,
                                        Z_COLS - N_BRANCH * D_MODEL - COL_MERGE * BRANCH_WIDTH)))
    w_up_b = w_up.astype(BF16)
    w_o_b = w_o.astype(BF16)
    w_g_b = w_ple_gate.astype(BF16)
    w_p_b = w_ple_proj.astype(BF16)

    xf = x.reshape(t, D_MODEL)
    p_rows = p.reshape(depth, t, PLE_DIM)
    for li in range(depth):
        z = _in_proj(xf, norm_mix[li][None], w_z, mb_cols[li], li)
        z3 = z.reshape(bn, s, Z_COLS)
        qa, ka, va, edge, o_c = _seq_mixers(z3, fbias[li][None], gq[li][None], gk[li][None],
                                            lower_bounds[li][None], hgrn_norm[li][None])
        cfirst = edge[:, :, 0, :N_HEADS].transpose(0, 2, 1).reshape(-1) + skip_slack[li]
        clast = edge[:, :, 1, :N_HEADS].transpose(0, 2, 1).reshape(-1)
        o_b = _fox_attn(qa, ka, va, cfirst, clast).reshape(t, BRANCH_WIDTH)
        xf = _merge(z, o_b, o_c.reshape(t, BRANCH_WIDTH), xf, p_rows, s, li,
                    conv_w[li], conv_b[li][None], sgu_norm[li][None], ws, bs,
                    w_up_b, w_o_b, norm_ple[li][None], w_g_b, w_p_b)
    return xf.reshape(bn, s, D_MODEL).astype(x.dtype)
```

```python
import functools
import math

import jax
import jax.numpy as jnp
import numpy as np
from jax import lax
from jax.experimental import pallas as pl
from jax.experimental.pallas import tpu as pltpu

F32 = jnp.float32
BF16 = jnp.bfloat16

D_MODEL = 1024
PLE_DIM = 256
N_BRANCH = 4
BRANCH_WIDTH = 256
HEAD_DIM = 64
N_HEADS = BRANCH_WIDTH // HEAD_DIM
CONV_WIDTH = 3
GLA_CHUNK = 64
SPATIAL_CHUNK = 128
EPS = 1e-6
MASK_VALUE = -1e30
LOG2E = math.log2(math.e)

Z_COLS = 8192
COL_MERGE = 0
COL_A = 16
COL_B = 20
COL_C = 24
COL_D = 28
COL_F = 31
SILU_COLS = (COL_A + 3, COL_B + 3, COL_C, COL_C + 3, COL_D + 2)

AUG = 128
LANE_C = HEAD_DIM
IN_PROJ_ROWS = 1024
IN_PROJ_COLS = 2048
MERGE_ROWS = 512
ATTN_BLOCK = 512
ATTN_STREAMS = 4
ATTN_WIDE = 2
ATTN_CUTOFF_LOG2 = 152.0
BF16_NORM_MARGIN = 1.02

VMEM_LIMIT_BYTES = 56 * 1024 * 1024
HGRN_SAFE_RANGE = 60.0
HGRN_LOOKAHEAD = 2


def _attn_block(seq_len):
    return min(ATTN_BLOCK, seq_len)


def _cparams(semantics):
    return pltpu.CompilerParams(dimension_semantics=semantics, vmem_limit_bytes=VMEM_LIMIT_BYTES)


def _split3(x):
    a = x.astype(BF16)
    r = x - a.astype(F32)
    b = r.astype(BF16)
    c = (r - b.astype(F32)).astype(BF16)
    return a, b, c


def _pack3(x):
    a, b, c = (piece.astype(F32) for piece in _split3(x))
    lane = _iota(x.shape, 1)
    packed = jnp.where(
        lane < N_HEADS, a,
        jnp.where(lane < 2 * N_HEADS, pltpu.roll(b, N_HEADS, axis=1),
                  jnp.where(lane < 3 * N_HEADS, pltpu.roll(c, 2 * N_HEADS, axis=1), 0.0)))
    return packed.astype(BF16)


def _dot(a, b):
    return jnp.dot(a, b, preferred_element_type=F32)


def _dot_nt(a, b):
    return lax.dot_general(a, b, (((1,), (1,)), ((), ())), preferred_element_type=F32)


def _dot_tn(a, b):
    return lax.dot_general(a, b, (((0,), (0,)), ((), ())), preferred_element_type=F32)


def _iota(shape, dim):
    return lax.broadcasted_iota(jnp.int32, shape, dim)


def _head_block_mask(rows, cols):
    return (_iota((rows, cols), 0) // HEAD_DIM) == (_iota((rows, cols), 1) // HEAD_DIM)


def _group_mean_sq(x):
    w = x.shape[-1]
    avg = jnp.where(_head_block_mask(w, w), 1.0 / HEAD_DIM, 0.0).astype(BF16)
    sq = x * x
    hi = sq.astype(BF16)
    lo = (sq - hi.astype(F32)).astype(BF16)
    return _dot(hi, avg) + _dot(lo, avg)


def _in_proj_kernel(x_ref, g_ref, w_ref, mb_ref, z_ref, h_ref):
    j = pl.program_id(1)
    tn = z_ref.shape[1]
    blocks_per_step = tn // BRANCH_WIDTH
    n_steps = Z_COLS // tn

    @pl.when(j == 0)
    def _():
        x = x_ref[...]
        ms = jnp.mean(x * x, axis=-1, keepdims=True)
        h_ref[...] = (x * lax.rsqrt(ms + EPS) * g_ref[...]).astype(BF16)

    n_merge_steps = (COL_A - COL_MERGE) // blocks_per_step

    @pl.when(j < n_merge_steps)
    def _():
        res = _dot(h_ref[...], w_ref[...])
        half_logit = (0.5 * (res + mb_ref[...])).astype(z_ref.dtype)
        z_ref[...] = 0.5 * jnp.tanh(half_logit) + 0.5

    for step in range(n_merge_steps, n_steps):
        @pl.when(j == step)
        def _(step=step):
            res = _dot(h_ref[...], w_ref[...])
            for blk in range(blocks_per_step):
                piece = res[:, blk * BRANCH_WIDTH:(blk + 1) * BRANCH_WIDTH]
                if step * blocks_per_step + blk in SILU_COLS:
                    piece = jax.nn.silu(piece)
                z_ref[:, blk * BRANCH_WIDTH:(blk + 1) * BRANCH_WIDTH] = piece.astype(z_ref.dtype)


def _in_proj(x, g, w_all, mb, li):
    t = x.shape[0]
    tm = min(IN_PROJ_ROWS, t)
    tn = IN_PROJ_COLS
    assert (COL_A - COL_MERGE) * BRANCH_WIDTH % tn == 0
    return pl.pallas_call(
        _in_proj_kernel,
        out_shape=jax.ShapeDtypeStruct((t, Z_COLS), BF16),
        grid=(t // tm, Z_COLS // tn),
        in_specs=[
            pl.BlockSpec((tm, D_MODEL), lambda i, j: (i, 0)),
            pl.BlockSpec((1, D_MODEL), lambda i, j: (0, 0)),
            pl.BlockSpec((None, D_MODEL, tn), lambda i, j: (li, 0, j)),
            pl.BlockSpec((1, tn), lambda i, j: (0, j)),
        ],
        out_specs=pl.BlockSpec((tm, tn), lambda i, j: (i, j)),
        scratch_shapes=[pltpu.VMEM((tm, D_MODEL), BF16)],
        compiler_params=_cparams(("parallel", "arbitrary")),
        name="in_proj",
    )(x, g, w_all, mb)


def _attn_prep_body(q_ref, k_ref, v_ref, f_ref, fb_ref, gq_ref, gk_ref,
                    wq_ref, wk_ref, wv_ref, oq_ref, ok_ref, ov_ref,
                    qa_ref, ka_ref, va_ref, edge_ref, carry_ref):
    ts = q_ref.shape[1]

    f = f_ref[0][:, :AUG].astype(F32) + fb_ref[...]
    log_f = (jnp.minimum(f, 0.0) - jnp.log1p(jnp.exp(-jnp.abs(f)))) * LOG2E
    tril = (_iota((ts, ts), 0) >= _iota((ts, ts), 1)).astype(BF16)
    part = _dot(tril, _pack3(log_f))
    total = (part + pltpu.roll(part, AUG - N_HEADS, axis=1)
             + pltpu.roll(part, AUG - 2 * N_HEADS, axis=1))
    cum = jnp.where(_iota((ts, AUG), 1) < N_HEADS, total, 0.0) + carry_ref[...]
    carry_ref[...] = cum[ts - 1:ts, :]
    edge_row = _iota((8, AUG), 0)
    edge_ref[0, 0] = jnp.where(edge_row == 0, cum[0:1, :],
                               jnp.where(edge_row == 1, cum[ts - 1:ts, :], 0.0))

    def normed(ref, gain_ref):
        x = ref[0].astype(F32)
        return (x * lax.rsqrt(_group_mean_sq(x) + EPS) * gain_ref[...]).astype(BF16)

    ccat = _pack3(cum)
    qa = _dot(jnp.concatenate([normed(q_ref, gq_ref), ccat], axis=1), wq_ref[...]) + oq_ref[...]
    ka = _dot(jnp.concatenate([normed(k_ref, gk_ref), ccat], axis=1), wk_ref[...]) + ok_ref[...]
    va = _dot(v_ref[0], wv_ref[...]) + ov_ref[...]
    for h in range(N_HEADS):
        lanes = slice(h * AUG, (h + 1) * AUG)
        qa_ref[0, h] = qa[:, lanes].astype(BF16)
        ka_ref[0, h] = ka[:, lanes].astype(BF16)
        va_ref[0, h] = va[:, lanes].astype(BF16)


def _attn_layout_constants():
    w, wide = BRANCH_WIDTH, N_HEADS * AUG
    place = np.zeros((w, wide), np.float32)
    sel_q = np.zeros((AUG, wide), np.float32)
    sel_k = np.zeros((AUG, wide), np.float32)
    ones_q = np.zeros((1, wide), np.float32)
    ones_k = np.zeros((1, wide), np.float32)
    ones_v = np.zeros((1, wide), np.float32)
    for h in range(N_HEADS):
        for d in range(HEAD_DIM):
            place[h * HEAD_DIM + d, h * AUG + d] = 1.0
        for i in range(3):
            sel_q[i * N_HEADS + h, h * AUG + LANE_C + i] = 1.0
            sel_k[i * N_HEADS + h, h * AUG + LANE_C + 3 + i] = -1.0
            ones_k[0, h * AUG + LANE_C + i] = 1.0
            ones_q[0, h * AUG + LANE_C + 3 + i] = 1.0
        ones_v[0, h * AUG + HEAD_DIM] = 1.0
    bf = lambda a: jnp.asarray(a, BF16)
    return (bf(np.concatenate([place, sel_q])), bf(np.concatenate([place, sel_k])), bf(place),
            jnp.asarray(ones_q), jnp.asarray(ones_k), jnp.asarray(ones_v))


def _fox_attn_kernel(cfirst_ref, clast_ref, qa_ref, ka_ref, va_ref, o_ref, m_ref, acc_ref):
    bi = pl.program_id(0)
    qi = pl.program_id(1)
    nblk = pl.num_programs(1)
    tq = qa_ref.shape[2]
    tr = tq // ATTN_STREAMS

    def process(h, kj, nblocks, diagonal):
        start = pl.multiple_of(kj * tq, tq)
        widths = [(nblocks - 1) * tq + (r + 1) * tr if diagonal else nblocks * tq
                  for r in range(ATTN_STREAMS)]
        scores = [
            _dot_nt(qa_ref[0, h, r * tr:(r + 1) * tr, :],
                    ka_ref[0, h, pl.ds(start, widths[r]), :])
            for r in range(ATTN_STREAMS)]
        for r, s in enumerate(scores):
            wk = widths[r]
            if diagonal:
                edge = s[:, wk - tr:]
                edge = jnp.where(_iota((tr, tr), 1) <= _iota((tr, tr), 0), edge, MASK_VALUE)
                s = edge if wk == tr else jnp.concatenate([s[:, :wk - tr], edge], axis=1)
            row_max = jnp.max(s, axis=-1, keepdims=True)
            if diagonal:
                m_new = jnp.broadcast_to(row_max, (tr, AUG))
            else:
                m_prev = m_ref[h, r]
                m_new = jnp.maximum(m_prev, row_max)
            p = jnp.concatenate(
                [jnp.exp2(s[:, c * AUG:(c + 1) * AUG] - m_new) for c in range(wk // AUG)], axis=1)
            pv = _dot(p.astype(BF16), va_ref[0, h, pl.ds(start, wk), :])
            acc_ref[h, r] = pv if diagonal else acc_ref[h, r] * jnp.exp2(m_prev - m_new) + pv
            m_ref[h, r] = m_new

    @pl.when(qi > 0)
    def _():
        for h in range(N_HEADS):
            process(h, qi - 1, 2, True)

    @pl.when(qi == 0)
    def _():
        for h in range(N_HEADS):
            process(h, qi, 1, True)

    last_off = jnp.maximum(qi - 1, 0)
    for h in range(N_HEADS):
        base = (bi * N_HEADS + h) * nblk
        c_i = cfirst_ref[base + qi]
        j_lo = lax.while_loop(
            lambda j: (j > 0) & (c_i - clast_ref[base + jnp.maximum(j - 1, 0)] >= 0.0),
            lambda j: j - 1, last_off)
        n_wide = (last_off - j_lo) // ATTN_WIDE

        def wide_body(t, carry, h=h, j_lo=j_lo):
            process(h, j_lo + t * ATTN_WIDE, ATTN_WIDE, False)
            return carry

        def single_body(j, carry, h=h):
            process(h, j, 1, False)
            return carry

        lax.fori_loop(0, n_wide, wide_body, 0)
        lax.fori_loop(j_lo + n_wide * ATTN_WIDE, last_off, single_body, 0)

    outs = []
    for h in range(N_HEADS):
        acc = acc_ref[h].reshape(tq, AUG)
        outs.append(acc[:, :HEAD_DIM] / acc[:, HEAD_DIM:HEAD_DIM + 1])
    o_ref[0] = jnp.concatenate(outs, axis=-1)


def _fox_attn(qa, ka, va, cfirst, clast):
    b, _, s, _ = qa.shape
    tq = _attn_block(s)
    q_spec = pl.BlockSpec((1, N_HEADS, tq, AUG), lambda bi, i, cf, cl: (bi, 0, i, 0))
    kv_spec = pl.BlockSpec((1, N_HEADS, s, AUG), lambda bi, i, cf, cl: (bi, 0, 0, 0))
    return pl.pallas_call(
        _fox_attn_kernel,
        out_shape=jax.ShapeDtypeStruct((b, s, BRANCH_WIDTH), F32),
        grid_spec=pltpu.PrefetchScalarGridSpec(
            num_scalar_prefetch=2,
            grid=(b, s // tq),
            in_specs=[q_spec, kv_spec, kv_spec],
            out_specs=pl.BlockSpec((1, tq, BRANCH_WIDTH), lambda bi, i, cf, cl: (bi, i, 0)),
            scratch_shapes=[pltpu.VMEM((N_HEADS, ATTN_STREAMS, tq // ATTN_STREAMS, AUG), F32),
                            pltpu.VMEM((N_HEADS, ATTN_STREAMS, tq // ATTN_STREAMS, AUG), F32)],
        ),
        compiler_params=_cparams(("parallel", "arbitrary")),
        name="fox_attn",
    )(cfirst, clast, qa, ka, va)


def _hgrn_body(q_ref, f_ref, i_ref, lb_ref, gain_ref, o_ref,
               state_ref, qs_ref, ks_ref, vs_ref, lg_ref, oc_ref):
    tc = q_ref.shape[1]
    c = GLA_CHUNK
    half = c // 2
    w = BRANCH_WIDTH
    n = tc // c
    lb = lb_ref[...]
    bd_mask = _head_block_mask(w, w)

    q = q_ref[0].astype(F32)
    fl = f_ref[0].astype(F32)
    v = i_ref[0].astype(F32)
    t_small = jnp.exp(-jnp.abs(fl))
    s_big = 1.0 / (1.0 + t_small)
    s_small = t_small * s_big
    pos = fl >= 0.0
    log_g = jnp.log(lb + (1.0 - lb) * jnp.where(pos, s_big, s_small))
    kf = (1.0 - lb) * jnp.where(pos, s_small, s_big)

    r_i, c_i = _iota((tc, tc), 0), _iota((tc, tc), 1)
    tril = ((r_i >= c_i) & (r_i // c == c_i // c)).astype(BF16)
    l1, l2, l3 = _split3(log_g)
    bsum = _dot(tril, l1) + _dot(tril, l2) + _dot(tril, l3)

    b3 = bsum.reshape(n, c, w)
    b_first, b_q1 = b3[:, 0:1], b3[:, half // 2 - 1:half // 2]
    b_mid, b_mid1 = b3[:, half - 1:half], b3[:, half:half + 1]
    b_q3, b_last = b3[:, half + half // 2 - 1:half + half // 2], b3[:, c - 1:c]
    spread = jnp.maximum(jnp.maximum(b_first - b_q1, b_q1 - b_mid),
                         jnp.maximum(b_mid1 - b_q3, b_q3 - b_last))
    safe = jnp.max(spread) <= HGRN_SAFE_RANGE

    @pl.when(safe)
    def _():
        top = _iota((n, c, w), 1) < half
        ref_d = jnp.where(top, b_q1, b_q3)
        q3, k3 = q.reshape(n, c, w), kf.reshape(n, c, w)
        qd = q3 * jnp.exp(b3 - ref_d)
        kd = k3 * jnp.exp(ref_d - b3)
        qo = jnp.where(top, 0.0, q3 * jnp.exp(jnp.minimum(b3 - b_mid, 0.0)))
        ko = jnp.where(top, k3 * jnp.exp(jnp.minimum(b_mid - b3, 0.0)), 0.0)
        qcat = jnp.concatenate(
            [qo, jnp.where(top, qd, 0.0), jnp.where(top, 0.0, qd)], axis=2).astype(BF16)
        kparts = [x.astype(BF16) for x in (ko, jnp.where(top, kd, 0.0), jnp.where(top, 0.0, kd))]
        q_in = (q3 * jnp.exp(b3)).astype(BF16)
        k_end = (k3 * jnp.exp(b_last - b3)).astype(BF16)
        e_last = jnp.exp(b_last)
        v_b = v.astype(BF16).reshape(n, c, w)
        causal = (_iota((c, w), 1) % HEAD_DIM) <= _iota((c, w), 0)

        def block_diag(x):
            return jnp.where(bd_mask, jnp.concatenate([x] * N_HEADS, axis=0), 0.0)

        def score(ci):
            kcat = jnp.concatenate([block_diag(kp[ci]) for kp in kparts], axis=1)
            return jnp.where(causal, _dot_nt(qcat[ci], kcat), 0.0).astype(BF16)

        def update(ci):
            return jnp.where(bd_mask, _dot_tn(v_b[ci], k_end[ci]), 0.0)

        ahead = HGRN_LOOKAHEAD
        scores = {ci: score(ci) for ci in range(min(ahead, n))}
        updates = {ci: update(ci) for ci in range(min(ahead, n))}
        state_t = state_ref[...]
        outs = []
        for ci in range(n):
            if ci + ahead < n:
                scores[ci + ahead] = score(ci + ahead)
                updates[ci + ahead] = update(ci + ahead)
            o_intra = _dot(scores.pop(ci), block_diag(v_b[ci]))
            outs.append(o_intra + _dot_nt(q_in[ci], state_t.astype(BF16)))
            state_t = state_t * e_last[ci] + updates.pop(ci)
        state_ref[...] = state_t
        oc_ref[...] = jnp.concatenate(outs, axis=0)

    @pl.when(jnp.logical_not(safe))
    def _():
        qs_ref[...] = q
        ks_ref[...] = kf
        vs_ref[...] = v
        lg_ref[...] = log_g

        def body(t, carry):
            g_t = jnp.exp(lg_ref[pl.ds(t, 1), :])
            outer = _dot_tn(vs_ref[pl.ds(t, 1), :].astype(BF16),
                            ks_ref[pl.ds(t, 1), :].astype(BF16))
            st = state_ref[...] * g_t + jnp.where(bd_mask, outer, 0.0)
            state_ref[...] = st
            oc_ref[pl.ds(t, 1), :] = _dot_nt(qs_ref[pl.ds(t, 1), :].astype(BF16), st.astype(BF16))
            return carry

        lax.fori_loop(0, tc, body, 0)

    o = oc_ref[...]
    o_ref[0] = o * lax.rsqrt(_group_mean_sq(o) + EPS) * gain_ref[...]


N_PREP_IN = 13
N_HGRN_IN = 5


def _seq_mixers_kernel(*refs):
    prep_in = refs[:N_PREP_IN]
    hgrn_in = refs[N_PREP_IN:N_PREP_IN + N_HGRN_IN]
    qa_ref, ka_ref, va_ref, edge_ref, o_ref = refs[N_PREP_IN + N_HGRN_IN:N_PREP_IN + N_HGRN_IN + 5]
    carry_ref, state_ref, qs_ref, ks_ref, vs_ref, lg_ref, oc_ref = refs[N_PREP_IN + N_HGRN_IN + 5:]

    @pl.when(pl.program_id(1) == 0)
    def _():
        carry_ref[...] = jnp.zeros_like(carry_ref)
        state_ref[...] = jnp.zeros_like(state_ref)

    _attn_prep_body(*prep_in, qa_ref, ka_ref, va_ref, edge_ref, carry_ref)
    _hgrn_body(*hgrn_in, o_ref, state_ref, qs_ref, ks_ref, vs_ref, lg_ref, oc_ref)


def _seq_mixers(z3, fbias, gq, gk, lb, gain):
    b, s, _ = z3.shape
    ts = _attn_block(s)
    col = lambda c: pl.BlockSpec((1, ts, BRANCH_WIDTH), lambda bi, si, c=c: (bi, si, c))
    full = lambda a: pl.BlockSpec(a.shape, lambda bi, si: (0, 0))
    aug = pl.BlockSpec((1, N_HEADS, ts, AUG), lambda bi, si: (bi, 0, si, 0))
    aug_shape = jax.ShapeDtypeStruct((b, N_HEADS, s, AUG), BF16)
    edge_shape = jax.ShapeDtypeStruct((b, s // ts, 8, AUG), F32)
    prep_consts = (fbias, gq, gk) + _attn_layout_constants()
    prep_specs = [col(COL_B), col(COL_B + 1), col(COL_B + 2), col(COL_F)] + [full(a) for a in prep_consts]
    hgrn_specs = [col(COL_C), col(COL_C + 1), col(COL_C + 2), full(lb), full(gain)]
    assert len(prep_specs) == N_PREP_IN and len(hgrn_specs) == N_HGRN_IN
    tile = pltpu.VMEM((ts, BRANCH_WIDTH), F32)
    return pl.pallas_call(
        _seq_mixers_kernel,
        out_shape=(aug_shape, aug_shape, aug_shape, edge_shape,
                   jax.ShapeDtypeStruct((b, s, BRANCH_WIDTH), F32)),
        grid=(b, s // ts),
        in_specs=prep_specs + hgrn_specs,
        out_specs=(aug, aug, aug, pl.BlockSpec((1, 1, 8, AUG), lambda bi, si: (bi, si, 0, 0)),
                   pl.BlockSpec((1, ts, BRANCH_WIDTH), lambda bi, si: (bi, si, 0))),
        scratch_shapes=[pltpu.VMEM((1, AUG), F32),
                        pltpu.VMEM((BRANCH_WIDTH, BRANCH_WIDTH), F32),
                        tile, tile, tile, tile, tile],
        compiler_params=_cparams(("parallel", "arbitrary")),
        name="seq_mixers",
    )(z3, z3, z3, z3, *prep_consts, z3, z3, z3, lb, gain)


HALO = 16


def _merge_kernel(zm_ref, za_ref, du_ref, dv_ref, dg_ref, bg_ref, cg_ref, hx_ref, hc_ref, ob_ref, oc_ref,
                  x_ref, p_ref, cw_ref, cb_ref, gv_ref, ws_ref, bs_ref, wup_ref,
                  wo_ref, gp_ref, wg_ref, wp_ref, out_ref, *, tiles_per_seq):
    tm = x_ref.shape[0]
    w = BRANCH_WIDTH

    za = za_ref[...].astype(F32)
    zc = za[:, 2 * w:3 * w] * za[:, 0:w]
    halo = hc_ref[...].astype(F32) * hx_ref[...].astype(F32)
    halo = jnp.where(pl.program_id(0) % tiles_per_seq == 0, 0.0, halo)
    ext = jnp.concatenate([halo, zc], axis=0)
    conv = zc * cw_ref[2:3, :]
    for tap in range(CONV_WIDTH - 1):
        shift = CONV_WIDTH - 1 - tap
        conv = conv + pltpu.roll(ext, shift, axis=0)[HALO:] * cw_ref[tap:tap + 1, :]
    y_a = za[:, w:2 * w] * (conv + cb_ref[...]) * za[:, 3 * w:4 * w]

    y_b = ob_ref[...] * bg_ref[...].astype(F32)
    y_c = oc_ref[...] * cg_ref[...].astype(F32)

    vd = dv_ref[...].astype(F32)
    vn = (vd * lax.rsqrt(_group_mean_sq(vd) + EPS) * gv_ref[...])
    lane_head = _iota((SPATIAL_CHUNK, w), 1) // HEAD_DIM
    mixed = []
    for ci in range(tm // SPATIAL_CHUNK):
        vc = vn[ci * SPATIAL_CHUNK:(ci + 1) * SPATIAL_CHUNK]
        stacked = jnp.concatenate(
            [jnp.where(lane_head == g, vc, 0.0) for g in range(N_HEADS)], axis=0).astype(BF16)
        mixed.append(_dot(ws_ref[...], stacked) + bs_ref[...])
    y_d = du_ref[...].astype(F32) * jnp.concatenate(mixed, axis=0) * dg_ref[...].astype(F32)

    merged = None
    for bi, y in enumerate((y_a, y_b, y_c, y_d)):
        gate = zm_ref[:, bi * D_MODEL:(bi + 1) * D_MODEL].astype(F32)
        term = gate * _dot(y.astype(BF16), wup_ref[bi])
        merged = term if merged is None else merged + term
    x1 = x_ref[...] + _dot(merged.astype(BF16), wo_ref[...])

    ms = jnp.mean(x1 * x1, axis=-1, keepdims=True)
    hp = (x1 * lax.rsqrt(ms + EPS) * gp_ref[...]).astype(BF16)
    ple = _dot(p_ref[...].astype(BF16), wp_ref[...])
    out_ref[...] = x1 + jax.nn.sigmoid(_dot(hp, wg_ref[...])) * ple


def _merge(z, o_b, o_c, x, p, seq_len, li, cw, cb, gv, ws, bs, wup, wo, gp, wg, wp):
    t = x.shape[0]
    tm = min(MERGE_ROWS, seq_len)
    w = BRANCH_WIDTH
    hstep = tm // HALO
    row = lambda width, c: pl.BlockSpec((tm, width), lambda i, c=c: (i, c))
    halo = lambda c: pl.BlockSpec((HALO, w), lambda i, c=c: (jnp.maximum(i * hstep - 1, 0), c))

    stacked = {id(a) for a in (ws, bs, wup, wo, wg, wp)}

    def full(a):
        if id(a) in stacked:
            return pl.BlockSpec((None,) + a.shape[1:], lambda i, n=a.ndim: (li,) + (0,) * (n - 1))
        return pl.BlockSpec(a.shape, lambda i, n=a.ndim: (0,) * n)

    consts = (cw, cb, gv, ws, bs, wup, wo, gp, wg, wp)
    return pl.pallas_call(
        functools.partial(_merge_kernel, tiles_per_seq=seq_len // tm),
        out_shape=jax.ShapeDtypeStruct((t, D_MODEL), F32),
        grid=(t // tm,),
        in_specs=[
            row(N_BRANCH * D_MODEL, COL_MERGE),
            row(4 * w, COL_A // 4),
            row(w, COL_D), row(w, COL_D + 1), row(w, COL_D + 2),
            row(w, COL_B + 3), row(w, COL_C + 3),
            halo(COL_A), halo(COL_A + 2),
            row(w, 0), row(w, 0),
            row(D_MODEL, 0),
            pl.BlockSpec((None, tm, PLE_DIM), lambda i: (li, i, 0)),
        ] + [full(a) for a in consts],
        out_specs=pl.BlockSpec((tm, D_MODEL), lambda i: (i, 0)),
        compiler_params=_cparams(("parallel",)),
        name="merge",
    )(z, z, z, z, z, z, z, z, z, o_b, o_c, x, p, *consts)


def _reorder_w_in(w_in):
    w = BRANCH_WIDTH
    n_pre = 8 * w
    n_f = N_HEADS
    n_cd = 7 * w
    wb = w_in.astype(BF16)
    pre = wb[:, :, :n_pre]
    fcols = wb[:, :, n_pre:n_pre + n_f]
    cd = wb[:, :, n_pre + n_f:n_pre + n_f + n_cd]
    merge = wb[:, :, n_pre + n_f + n_cd:]
    pad = jnp.zeros(w_in.shape[:2] + (w - n_f,), BF16)
    return jnp.concatenate([merge, pre, cd, fcols, pad], axis=-1)


def kernel(x, p, norm_mix, w_in, conv_w, conv_b, fgate_bias, q_norm, k_norm, lb_logits,
           hgrn_norm, sgu_norm, spatial_w, spatial_b, w_up, merge_b, w_o, norm_ple,
           w_ple_gate, w_ple_proj):
    bn, s, _ = x.shape
    depth = w_in.shape[0]
    t = bn * s
    assert w_in.shape[-1] == 15 * BRANCH_WIDTH + N_HEADS + N_BRANCH * D_MODEL
    assert s % SPATIAL_CHUNK == 0 and s % GLA_CHUNK == 0

    lb_p = jax.nn.softmax(lb_logits.astype(F32), axis=0)
    lower_bounds = jnp.clip(jnp.cumsum(lb_p, axis=0) - lb_p[0], 0.0, 1.0)
    w_z = _reorder_w_in(w_in)
    fbias = jnp.pad(fgate_bias.astype(F32), ((0, 0), (0, AUG - N_HEADS)))
    gq = jnp.tile(q_norm.astype(F32) * (HEAD_DIM ** -0.5 * LOG2E), (1, N_HEADS))
    gk = jnp.tile(k_norm.astype(F32), (1, N_HEADS))
    qk_bound = (HEAD_DIM * BF16_NORM_MARGIN) * jnp.max(jnp.abs(gq), axis=1) * jnp.max(jnp.abs(gk), axis=1)
    skip_slack = 2.0 * qk_bound + ATTN_CUTOFF_LOG2
    causal = jnp.tril(jnp.ones((SPATIAL_CHUNK, SPATIAL_CHUNK), F32))
    ws = (spatial_w.astype(F32) * causal).transpose(0, 2, 1, 3).reshape(
        depth, SPATIAL_CHUNK, N_HEADS * SPATIAL_CHUNK).astype(BF16)
    bs = jnp.repeat(spatial_b.astype(F32).transpose(0, 2, 1), HEAD_DIM, axis=-1)
    mb_cols = jnp.pad(merge_b.astype(F32).reshape(depth, 1, N_BRANCH * D_MODEL),
                      ((0, 0), (0, 0), (COL_MERGE * BRANCH_WIDTH,
                                        Z_COLS - N_BRANCH * D_MODEL - COL_MERGE * BRANCH_WIDTH)))
    w_up_b = w_up.astype(BF16)
    w_o_b = w_o.astype(BF16)
    w_g_b = w_ple_gate.astype(BF16)
    w_p_b = w_ple_proj.astype(BF16)

    xf = x.reshape(t, D_MODEL)
    p_rows = p.reshape(depth, t, PLE_DIM)
    for li in range(depth):
        z = _in_proj(xf, norm_mix[li][None], w_z, mb_cols[li], li)
        z3 = z.reshape(bn, s, Z_COLS)
        qa, ka, va, edge, o_c = _seq_mixers(z3, fbias[li][None], gq[li][None], gk[li][None],
                                            lower_bounds[li][None], hgrn_norm[li][None])
        cfirst = edge[:, :, 0, :N_HEADS].transpose(0, 2, 1).reshape(-1) + skip_slack[li]
        clast = edge[:, :, 1, :N_HEADS].transpose(0, 2, 1).reshape(-1)
        o_b = _fox_attn(qa, ka, va, cfirst, clast).reshape(t, BRANCH_WIDTH)
        xf = _merge(z, o_b, o_c.reshape(t, BRANCH_WIDTH), xf, p_rows, s, li,
                    conv_w[li], conv_b[li][None], sgu_norm[li][None], ws, bs,
                    w_up_b, w_o_b, norm_ple[li][None], w_g_b, w_p_b)
    return xf.reshape(bn, s, D_MODEL).astype(x.dtype)
```

```python
import functools
import math

import jax
import jax.numpy as jnp
import numpy as np
from jax import lax
from jax.experimental import pallas as pl
from jax.experimental.pallas import tpu as pltpu

F32 = jnp.float32
BF16 = jnp.bfloat16

D_MODEL = 1024
PLE_DIM = 256
N_BRANCH = 4
BRANCH_WIDTH = 256
HEAD_DIM = 64
N_HEADS = BRANCH_WIDTH // HEAD_DIM
CONV_WIDTH = 3
GLA_CHUNK = 64
SPATIAL_CHUNK = 128
EPS = 1e-6
MASK_VALUE = -1e30
LOG2E = math.log2(math.e)

Z_COLS = 8192
COL_MERGE = 0
COL_A = 16
COL_B = 20
COL_C = 24
COL_D = 28
COL_F = 31
SILU_COLS = (COL_A + 3, COL_B + 3, COL_C, COL_C + 3, COL_D + 2)

AUG = 128
LANE_C = HEAD_DIM
IN_PROJ_ROWS = 1024
IN_PROJ_COLS = 2048
MERGE_ROWS = 512
ATTN_BLOCK = 512
ATTN_STREAMS = 4
ATTN_WIDE = 2
ATTN_CUTOFF_LOG2 = 152.0
BF16_NORM_MARGIN = 1.02

VMEM_LIMIT_BYTES = 56 * 1024 * 1024
HGRN_SAFE_RANGE = 60.0
HGRN_LOOKAHEAD = 2


def _attn_block(seq_len):
    return min(ATTN_BLOCK, seq_len)


def _cparams(semantics):
    return pltpu.CompilerParams(dimension_semantics=semantics, vmem_limit_bytes=VMEM_LIMIT_BYTES)


def _split3(x):
    a = x.astype(BF16)
    r = x - a.astype(F32)
    b = r.astype(BF16)
    c = (r - b.astype(F32)).astype(BF16)
    return a, b, c


def _pack3(x):
    a, b, c = (piece.astype(F32) for piece in _split3(x))
    lane = _iota(x.shape, 1)
    packed = jnp.where(
        lane < N_HEADS, a,
        jnp.where(lane < 2 * N_HEADS, pltpu.roll(b, N_HEADS, axis=1),
                  jnp.where(lane < 3 * N_HEADS, pltpu.roll(c, 2 * N_HEADS, axis=1), 0.0)))
    return packed.astype(BF16)


def _dot(a, b):
    return jnp.dot(a, b, preferred_element_type=F32)


def _dot_nt(a, b):
    return lax.dot_general(a, b, (((1,), (1,)), ((), ())), preferred_element_type=F32)


def _dot_tn(a, b):
    return lax.dot_general(a, b, (((0,), (0,)), ((), ())), preferred_element_type=F32)


def _iota(shape, dim):
    return lax.broadcasted_iota(jnp.int32, shape, dim)


def _head_block_mask(rows, cols):
    return (_iota((rows, cols), 0) // HEAD_DIM) == (_iota((rows, cols), 1) // HEAD_DIM)


def _group_mean_sq(x):
    w = x.shape[-1]
    avg = jnp.where(_head_block_mask(w, w), 1.0 / HEAD_DIM, 0.0).astype(BF16)
    sq = x * x
    hi = sq.astype(BF16)
    lo = (sq - hi.astype(F32)).astype(BF16)
    return _dot(hi, avg) + _dot(lo, avg)


def _in_proj_kernel(x_ref, g_ref, w_ref, mb_ref, z_ref, h_ref):
    j = pl.program_id(1)
    tn = z_ref.shape[1]
    blocks_per_step = tn // BRANCH_WIDTH
    n_steps = Z_COLS // tn

    @pl.when(j == 0)
    def _():
        x = x_ref[...]
        ms = jnp.mean(x * x, axis=-1, keepdims=True)
        h_ref[...] = (x * lax.rsqrt(ms + EPS) * g_ref[...]).astype(BF16)

    n_merge_steps = (COL_A - COL_MERGE) // blocks_per_step

    @pl.when(j < n_merge_steps)
    def _():
        res = _dot(h_ref[...], w_ref[...])
        half_logit = (0.5 * (res + mb_ref[...])).astype(z_ref.dtype)
        z_ref[...] = 0.5 * jnp.tanh(half_logit) + 0.5

    for step in range(n_merge_steps, n_steps):
        @pl.when(j == step)
        def _(step=step):
            res = _dot(h_ref[...], w_ref[...])
            for blk in range(blocks_per_step):
                piece = res[:, blk * BRANCH_WIDTH:(blk + 1) * BRANCH_WIDTH]
                if step * blocks_per_step + blk in SILU_COLS:
                    piece = jax.nn.silu(piece)
                z_ref[:, blk * BRANCH_WIDTH:(blk + 1) * BRANCH_WIDTH] = piece.astype(z_ref.dtype)


def _in_proj(x, g, w_all, mb, li):
    t = x.shape[0]
    tm = min(IN_PROJ_ROWS, t)
    tn = IN_PROJ_COLS
    assert (COL_A - COL_MERGE) * BRANCH_WIDTH % tn == 0
    return pl.pallas_call(
        _in_proj_kernel,
        out_shape=jax.ShapeDtypeStruct((t, Z_COLS), BF16),
        grid=(t // tm, Z_COLS // tn),
        in_specs=[
            pl.BlockSpec((tm, D_MODEL), lambda i, j: (i, 0)),
            pl.BlockSpec((1, D_MODEL), lambda i, j: (0, 0)),
            pl.BlockSpec((None, D_MODEL, tn), lambda i, j: (li, 0, j)),
            pl.BlockSpec((1, tn), lambda i, j: (0, j)),
        ],
        out_specs=pl.BlockSpec((tm, tn), lambda i, j: (i, j)),
        scratch_shapes=[pltpu.VMEM((tm, D_MODEL), BF16)],
        compiler_params=_cparams(("parallel", "arbitrary")),
        name="in_proj",
    )(x, g, w_all, mb)


def _attn_prep_body(q_ref, k_ref, v_ref, f_ref, fb_ref, gq_ref, gk_ref,
                    wq_ref, wk_ref, wv_ref, oq_ref, ok_ref, ov_ref,
                    qa_ref, ka_ref, va_ref, edge_ref, carry_ref):
    ts = q_ref.shape[1]

    f = f_ref[0][:, :AUG].astype(F32) + fb_ref[...]
    log_f = (jnp.minimum(f, 0.0) - jnp.log1p(jnp.exp(-jnp.abs(f)))) * LOG2E
    tril = (_iota((ts, ts), 0) >= _iota((ts, ts), 1)).astype(BF16)
    part = _dot(tril, _pack3(log_f))
    total = (part + pltpu.roll(part, AUG - N_HEADS, axis=1)
             + pltpu.roll(part, AUG - 2 * N_HEADS, axis=1))
    cum = jnp.where(_iota((ts, AUG), 1) < N_HEADS, total, 0.0) + carry_ref[...]
    carry_ref[...] = cum[ts - 1:ts, :]
    edge_row = _iota((8, AUG), 0)
    edge_ref[0, 0] = jnp.where(edge_row == 0, cum[0:1, :],
                               jnp.where(edge_row == 1, cum[ts - 1:ts, :], 0.0))

    def normed(ref, gain_ref):
        x = ref[0].astype(F32)
        return (x * lax.rsqrt(_group_mean_sq(x) + EPS) * gain_ref[...]).astype(BF16)

    ccat = _pack3(cum)
    qa = _dot(jnp.concatenate([normed(q_ref, gq_ref), ccat], axis=1), wq_ref[...]) + oq_ref[...]
    ka = _dot(jnp.concatenate([normed(k_ref, gk_ref), ccat], axis=1), wk_ref[...]) + ok_ref[...]
    va = _dot(v_ref[0], wv_ref[...]) + ov_ref[...]
    for h in range(N_HEADS):
        lanes = slice(h * AUG, (h + 1) * AUG)
        qa_ref[0, h] = qa[:, lanes].astype(BF16)
        ka_ref[0, h] = ka[:, lanes].astype(BF16)
        va_ref[0, h] = va[:, lanes].astype(BF16)


def _attn_layout_constants():
    w, wide = BRANCH_WIDTH, N_HEADS * AUG
    place = np.zeros((w, wide), np.float32)
    sel_q = np.zeros((AUG, wide), np.float32)
    sel_k = np.zeros((AUG, wide), np.float32)
    ones_q = np.zeros((1, wide), np.float32)
    ones_k = np.zeros((1, wide), np.float32)
    ones_v = np.zeros((1, wide), np.float32)
    for h in range(N_HEADS):
        for d in range(HEAD_DIM):
            place[h * HEAD_DIM + d, h * AUG + d] = 1.0
        for i in range(3):
            sel_q[i * N_HEADS + h, h * AUG + LANE_C + i] = 1.0
            sel_k[i * N_HEADS + h, h * AUG + LANE_C + 3 + i] = -1.0
            ones_k[0, h * AUG + LANE_C + i] = 1.0
            ones_q[0, h * AUG + LANE_C + 3 + i] = 1.0
        ones_v[0, h * AUG + HEAD_DIM] = 1.0
    bf = lambda a: jnp.asarray(a, BF16)
    return (bf(np.concatenate([place, sel_q])), bf(np.concatenate([place, sel_k])), bf(place),
            jnp.asarray(ones_q), jnp.asarray(ones_k), jnp.asarray(ones_v))


def _fox_attn_kernel(cfirst_ref, clast_ref, qa_ref, ka_ref, va_ref, o_ref, m_ref, acc_ref):
    bi = pl.program_id(0)
    qi = pl.program_id(1)
    nblk = pl.num_programs(1)
    tq = qa_ref.shape[2]
    tr = tq // ATTN_STREAMS

    def process(h, kj, nblocks, diagonal):
        start = pl.multiple_of(kj * tq, tq)
        widths = [(nblocks - 1) * tq + (r + 1) * tr if diagonal else nblocks * tq
                  for r in range(ATTN_STREAMS)]
        scores = [
            _dot_nt(qa_ref[0, h, r * tr:(r + 1) * tr, :],
                    ka_ref[0, h, pl.ds(start, widths[r]), :])
            for r in range(ATTN_STREAMS)]
        for r, s in enumerate(scores):
            wk = widths[r]
            if diagonal:
                edge = s[:, wk - tr:]
                edge = jnp.where(_iota((tr, tr), 1) <= _iota((tr, tr), 0), edge, MASK_VALUE)
                s = edge if wk == tr else jnp.concatenate([s[:, :wk - tr], edge], axis=1)
            row_max = jnp.max(s, axis=-1, keepdims=True)
            if diagonal:
                m_new = jnp.broadcast_to(row_max, (tr, AUG))
            else:
                m_prev = m_ref[h, r]
                m_new = jnp.maximum(m_prev, row_max)
            p = jnp.concatenate(
                [jnp.exp2(s[:, c * AUG:(c + 1) * AUG] - m_new) for c in range(wk // AUG)], axis=1)
            pv = _dot(p.astype(BF16), va_ref[0, h, pl.ds(start, wk), :])
            acc_ref[h, r] = pv if diagonal else acc_ref[h, r] * jnp.exp2(m_prev - m_new) + pv
            m_ref[h, r] = m_new

    @pl.when(qi > 0)
    def _():
        for h in range(N_HEADS):
            process(h, qi - 1, 2, True)

    @pl.when(qi == 0)
    def _():
        for h in range(N_HEADS):
            process(h, qi, 1, True)

    last_off = jnp.maximum(qi - 1, 0)
    for h in range(N_HEADS):
        base = (bi * N_HEADS + h) * nblk
        c_i = cfirst_ref[base + qi]
        j_lo = lax.while_loop(
            lambda j: (j > 0) & (c_i - clast_ref[base + jnp.maximum(j - 1, 0)] >= 0.0),
            lambda j: j - 1, last_off)
        n_wide = (last_off - j_lo) // ATTN_WIDE

        def wide_body(t, carry, h=h, j_lo=j_lo):
            process(h, j_lo + t * ATTN_WIDE, ATTN_WIDE, False)
            return carry

        def single_body(j, carry, h=h):
            process(h, j, 1, False)
            return carry

        lax.fori_loop(0, n_wide, wide_body, 0)
        lax.fori_loop(j_lo + n_wide * ATTN_WIDE, last_off, single_body, 0)

    outs = []
    for h in range(N_HEADS):
        acc = acc_ref[h].reshape(tq, AUG)
        outs.append(acc[:, :HEAD_DIM] / acc[:, HEAD_DIM:HEAD_DIM + 1])
    o_ref[0] = jnp.concatenate(outs, axis=-1)


def _fox_attn(qa, ka, va, cfirst, clast):
    b, _, s, _ = qa.shape
    tq = _attn_block(s)
    q_spec = pl.BlockSpec((1, N_HEADS, tq, AUG), lambda bi, i, cf, cl: (bi, 0, i, 0))
    kv_spec = pl.BlockSpec((1, N_HEADS, s, AUG), lambda bi, i, cf, cl: (bi, 0, 0, 0))
    return pl.pallas_call(
        _fox_attn_kernel,
        out_shape=jax.ShapeDtypeStruct((b, s, BRANCH_WIDTH), F32),
        grid_spec=pltpu.PrefetchScalarGridSpec(
            num_scalar_prefetch=2,
            grid=(b, s // tq),
            in_specs=[q_spec, kv_spec, kv_spec],
            out_specs=pl.BlockSpec((1, tq, BRANCH_WIDTH), lambda bi, i, cf, cl: (bi, i, 0)),
            scratch_shapes=[pltpu.VMEM((N_HEADS, ATTN_STREAMS, tq // ATTN_STREAMS, AUG), F32),
                            pltpu.VMEM((N_HEADS, ATTN_STREAMS, tq // ATTN_STREAMS, AUG), F32)],
        ),
        compiler_params=_cparams(("parallel", "arbitrary")),
        name="fox_attn",
    )(cfirst, clast, qa, ka, va)


def _hgrn_body(q_ref, f_ref, i_ref, lb_ref, gain_ref, o_ref,
               state_ref, qs_ref, ks_ref, vs_ref, lg_ref, oc_ref):
    tc = q_ref.shape[1]
    c = GLA_CHUNK
    half = c // 2
    w = BRANCH_WIDTH
    n = tc // c
    lb = lb_ref[...]
    bd_mask = _head_block_mask(w, w)

    q = q_ref[0].astype(F32)
    fl = f_ref[0].astype(F32)
    v = i_ref[0].astype(F32)
    t_small = jnp.exp(-jnp.abs(fl))
    s_big = 1.0 / (1.0 + t_small)
    s_small = t_small * s_big
    pos = fl >= 0.0
    log_g = jnp.log(lb + (1.0 - lb) * jnp.where(pos, s_big, s_small))
    kf = (1.0 - lb) * jnp.where(pos, s_small, s_big)

    r_i, c_i = _iota((tc, tc), 0), _iota((tc, tc), 1)
    tril = ((r_i >= c_i) & (r_i // c == c_i // c)).astype(BF16)
    l1, l2, l3 = _split3(log_g)
    bsum = _dot(tril, l1) + _dot(tril, l2) + _dot(tril, l3)

    b3 = bsum.reshape(n, c, w)
    b_first, b_q1 = b3[:, 0:1], b3[:, half // 2 - 1:half // 2]
    b_mid, b_mid1 = b3[:, half - 1:half], b3[:, half:half + 1]
    b_q3, b_last = b3[:, half + half // 2 - 1:half + half // 2], b3[:, c - 1:c]
    spread = jnp.maximum(jnp.maximum(b_first - b_q1, b_q1 - b_mid),
                         jnp.maximum(b_mid1 - b_q3, b_q3 - b_last))
    safe = jnp.max(spread) <= HGRN_SAFE_RANGE

    @pl.when(safe)
    def _():
        top = _iota((n, c, w), 1) < half
        ref_d = jnp.where(top, b_q1, b_q3)
        q3, k3 = q.reshape(n, c, w), kf.reshape(n, c, w)
        qd = q3 * jnp.exp(b3 - ref_d)
        kd = k3 * jnp.exp(ref_d - b3)
        qo = jnp.where(top, 0.0, q3 * jnp.exp(jnp.minimum(b3 - b_mid, 0.0)))
        ko = jnp.where(top, k3 * jnp.exp(jnp.minimum(b_mid - b3, 0.0)), 0.0)
        qcat = jnp.concatenate(
            [qo, jnp.where(top, qd, 0.0), jnp.where(top, 0.0, qd)], axis=2).astype(BF16)
        kparts = [x.astype(BF16) for x in (ko, jnp.where(top, kd, 0.0), jnp.where(top, 0.0, kd))]
        q_in = (q3 * jnp.exp(b3)).astype(BF16)
        k_end = (k3 * jnp.exp(b_last - b3)).astype(BF16)
        e_last = jnp.exp(b_last)
        v_b = v.astype(BF16).reshape(n, c, w)
        causal = (_iota((c, w), 1) % HEAD_DIM) <= _iota((c, w), 0)

        def block_diag(x):
            return jnp.where(bd_mask, jnp.concatenate([x] * N_HEADS, axis=0), 0.0)

        def score(ci):
            kcat = jnp.concatenate([block_diag(kp[ci]) for kp in kparts], axis=1)
            return jnp.where(causal, _dot_nt(qcat[ci], kcat), 0.0).astype(BF16)

        def update(ci):
            return jnp.where(bd_mask, _dot_tn(v_b[ci], k_end[ci]), 0.0)

        ahead = HGRN_LOOKAHEAD
        scores = {ci: score(ci) for ci in range(min(ahead, n))}
        updates = {ci: update(ci) for ci in range(min(ahead, n))}
        state_t = state_ref[...]
        outs = []
        for ci in range(n):
            if ci + ahead < n:
                scores[ci + ahead] = score(ci + ahead)
                updates[ci + ahead] = update(ci + ahead)
            o_intra = _dot(scores.pop(ci), block_diag(v_b[ci]))
            outs.append(o_intra + _dot_nt(q_in[ci], state_t.astype(BF16)))
            state_t = state_t * e_last[ci] + updates.pop(ci)
        state_ref[...] = state_t
        oc_ref[...] = jnp.concatenate(outs, axis=0)

    @pl.when(jnp.logical_not(safe))
    def _():
        qs_ref[...] = q
        ks_ref[...] = kf
        vs_ref[...] = v
        lg_ref[...] = log_g

        def body(t, carry):
            g_t = jnp.exp(lg_ref[pl.ds(t, 1), :])
            outer = _dot_tn(vs_ref[pl.ds(t, 1), :].astype(BF16),
                            ks_ref[pl.ds(t, 1), :].astype(BF16))
            st = state_ref[...] * g_t + jnp.where(bd_mask, outer, 0.0)
            state_ref[...] = st
            oc_ref[pl.ds(t, 1), :] = _dot_nt(qs_ref[pl.ds(t, 1), :].astype(BF16), st.astype(BF16))
            return carry

        lax.fori_loop(0, tc, body, 0)

    o = oc_ref[...]
    o_ref[0] = o * lax.rsqrt(_group_mean_sq(o) + EPS) * gain_ref[...]


N_PREP_IN = 13
N_HGRN_IN = 5


def _seq_mixers_kernel(*refs):
    prep_in = refs[:N_PREP_IN]
    hgrn_in = refs[N_PREP_IN:N_PREP_IN + N_HGRN_IN]
    qa_ref, ka_ref, va_ref, edge_ref, o_ref = refs[N_PREP_IN + N_HGRN_IN:N_PREP_IN + N_HGRN_IN + 5]
    carry_ref, state_ref, qs_ref, ks_ref, vs_ref, lg_ref, oc_ref = refs[N_PREP_IN + N_HGRN_IN + 5:]

    @pl.when(pl.program_id(1) == 0)
    def _():
        carry_ref[...] = jnp.zeros_like(carry_ref)
        state_ref[...] = jnp.zeros_like(state_ref)

    _attn_prep_body(*prep_in, qa_ref, ka_ref, va_ref, edge_ref, carry_ref)
    _hgrn_body(*hgrn_in, o_ref, state_ref, qs_ref, ks_ref, vs_ref, lg_ref, oc_ref)


def _seq_mixers(z3, fbias, gq, gk, lb, gain):
    b, s, _ = z3.shape
    ts = _attn_block(s)
    col = lambda c: pl.BlockSpec((1, ts, BRANCH_WIDTH), lambda bi, si, c=c: (bi, si, c))
    full = lambda a: pl.BlockSpec(a.shape, lambda bi, si: (0, 0))
    aug = pl.BlockSpec((1, N_HEADS, ts, AUG), lambda bi, si: (bi, 0, si, 0))
    aug_shape = jax.ShapeDtypeStruct((b, N_HEADS, s, AUG), BF16)
    edge_shape = jax.ShapeDtypeStruct((b, s // ts, 8, AUG), F32)
    prep_consts = (fbias, gq, gk) + _attn_layout_constants()
    prep_specs = [col(COL_B), col(COL_B + 1), col(COL_B + 2), col(COL_F)] + [full(a) for a in prep_consts]
    hgrn_specs = [col(COL_C), col(COL_C + 1), col(COL_C + 2), full(lb), full(gain)]
    assert len(prep_specs) == N_PREP_IN and len(hgrn_specs) == N_HGRN_IN
    tile = pltpu.VMEM((ts, BRANCH_WIDTH), F32)
    return pl.pallas_call(
        _seq_mixers_kernel,
        out_shape=(aug_shape, aug_shape, aug_shape, edge_shape,
                   jax.ShapeDtypeStruct((b, s, BRANCH_WIDTH), F32)),
        grid=(b, s // ts),
        in_specs=prep_specs + hgrn_specs,
        out_specs=(aug, aug, aug, pl.BlockSpec((1, 1, 8, AUG), lambda bi, si: (bi, si, 0, 0)),
                   pl.BlockSpec((1, ts, BRANCH_WIDTH), lambda bi, si: (bi, si, 0))),
        scratch_shapes=[pltpu.VMEM((1, AUG), F32),
                        pltpu.VMEM((BRANCH_WIDTH, BRANCH_WIDTH), F32),
                        tile, tile, tile, tile, tile],
        compiler_params=_cparams(("parallel", "arbitrary")),
        name="seq_mixers",
    )(z3, z3, z3, z3, *prep_consts, z3, z3, z3, lb, gain)


HALO = 16


def _merge_kernel(zm_ref, za_ref, du_ref, dv_ref, dg_ref, bg_ref, cg_ref, hx_ref, hc_ref, ob_ref, oc_ref,
                  x_ref, p_ref, cw_ref, cb_ref, gv_ref, ws_ref, bs_ref, wup_ref,
                  wo_ref, gp_ref, wg_ref, wp_ref, out_ref, *, tiles_per_seq):
    tm = x_ref.shape[0]
    w = BRANCH_WIDTH

    za = za_ref[...].astype(F32)
    zc = za[:, 2 * w:3 * w] * za[:, 0:w]
    halo = hc_ref[...].astype(F32) * hx_ref[...].astype(F32)
    halo = jnp.where(pl.program_id(0) % tiles_per_seq == 0, 0.0, halo)
    ext = jnp.concatenate([halo, zc], axis=0)
    conv = zc * cw_ref[2:3, :]
    for tap in range(CONV_WIDTH - 1):
        shift = CONV_WIDTH - 1 - tap
        conv = conv + pltpu.roll(ext, shift, axis=0)[HALO:] * cw_ref[tap:tap + 1, :]
    y_a = za[:, w:2 * w] * (conv + cb_ref[...]) * za[:, 3 * w:4 * w]

    y_b = ob_ref[...] * bg_ref[...].astype(F32)
    y_c = oc_ref[...] * cg_ref[...].astype(F32)

    vd = dv_ref[...].astype(F32)
    vn = (vd * lax.rsqrt(_group_mean_sq(vd) + EPS) * gv_ref[...])
    lane_head = _iota((SPATIAL_CHUNK, w), 1) // HEAD_DIM
    mixed = []
    for ci in range(tm // SPATIAL_CHUNK):
        vc = vn[ci * SPATIAL_CHUNK:(ci + 1) * SPATIAL_CHUNK]
        stacked = jnp.concatenate(
            [jnp.where(lane_head == g, vc, 0.0) for g in range(N_HEADS)], axis=0).astype(BF16)
        mixed.append(_dot(ws_ref[...], stacked) + bs_ref[...])
    y_d = du_ref[...].astype(F32) * jnp.concatenate(mixed, axis=0) * dg_ref[...].astype(F32)

    merged = None
    for bi, y in enumerate((y_a, y_b, y_c, y_d)):
        gate = zm_ref[:, bi * D_MODEL:(bi + 1) * D_MODEL].astype(F32)
        term = gate * _dot(y.astype(BF16), wup_ref[bi])
        merged = term if merged is None else merged + term
    x1 = x_ref[...] + _dot(merged.astype(BF16), wo_ref[...])

    ms = jnp.mean(x1 * x1, axis=-1, keepdims=True)
    hp = (x1 * lax.rsqrt(ms + EPS) * gp_ref[...]).astype(BF16)
    ple = _dot(p_ref[...].astype(BF16), wp_ref[...])
    out_ref[...] = x1 + jax.nn.sigmoid(_dot(hp, wg_ref[...])) * ple


def _merge(z, o_b, o_c, x, p, seq_len, li, cw, cb, gv, ws, bs, wup, wo, gp, wg, wp):
    t = x.shape[0]
    tm = min(MERGE_ROWS, seq_len)
    w = BRANCH_WIDTH
    hstep = tm // HALO
    row = lambda width, c: pl.BlockSpec((tm, width), lambda i, c=c: (i, c))
    halo = lambda c: pl.BlockSpec((HALO, w), lambda i, c=c: (jnp.maximum(i * hstep - 1, 0), c))

    stacked = {id(a) for a in (ws, bs, wup, wo, wg, wp)}

    def full(a):
        if id(a) in stacked:
            return pl.BlockSpec((None,) + a.shape[1:], lambda i, n=a.ndim: (li,) + (0,) * (n - 1))
        return pl.BlockSpec(a.shape, lambda i, n=a.ndim: (0,) * n)

    consts = (cw, cb, gv, ws, bs, wup, wo, gp, wg, wp)
    return pl.pallas_call(
        functools.partial(_merge_kernel, tiles_per_seq=seq_len // tm),
        out_shape=jax.ShapeDtypeStruct((t, D_MODEL), F32),
        grid=(t // tm,),
        in_specs=[
            row(N_BRANCH * D_MODEL, COL_MERGE),
            row(4 * w, COL_A // 4),
            row(w, COL_D), row(w, COL_D + 1), row(w, COL_D + 2),
            row(w, COL_B + 3), row(w, COL_C + 3),
            halo(COL_A), halo(COL_A + 2),
            row(w, 0), row(w, 0),
            row(D_MODEL, 0),
            pl.BlockSpec((None, tm, PLE_DIM), lambda i: (li, i, 0)),
        ] + [full(a) for a in consts],
        out_specs=pl.BlockSpec((tm, D_MODEL), lambda i: (i, 0)),
        compiler_params=_cparams(("parallel",)),
        name="merge",
    )(z, z, z, z, z, z, z, z, z, o_b, o_c, x, p, *consts)


RELAYOUT_COPY, RELAYOUT_SHIFT, RELAYOUT_HEAD = 0, 1, 2


def _relayout_kernel(src_ref, mode_ref, cur_ref, nxt_ref, out_ref):
    mode = mode_ref[pl.program_id(1)]
    w = BRANCH_WIDTH
    lane = _iota(cur_ref.shape, 1)
    cur = cur_ref[...]
    shifted = jnp.where(lane < w - N_HEADS,
                        pltpu.roll(cur, w - N_HEADS, axis=1),
                        pltpu.roll(nxt_ref[...], w - N_HEADS, axis=1))
    head = jnp.where(lane < N_HEADS, cur, 0.0)
    out = jnp.where(mode == RELAYOUT_SHIFT, shifted, jnp.where(mode == RELAYOUT_HEAD, head, cur))
    out_ref[...] = out.astype(out_ref.dtype)


def _reorder_w_in(w_in):
    depth = w_in.shape[0]
    n_blocks = Z_COLS // BRANCH_WIDTH
    src, mode = [], []
    for blk in range(n_blocks):
        if COL_MERGE <= blk < COL_MERGE + 16:
            src.append(15 + blk - COL_MERGE), mode.append(RELAYOUT_SHIFT)
        elif COL_A <= blk < COL_A + 8:
            src.append(blk - COL_A), mode.append(RELAYOUT_COPY)
        elif COL_C <= blk < COL_C + 7:
            src.append(8 + blk - COL_C), mode.append(RELAYOUT_SHIFT)
        else:
            assert blk == COL_F
            src.append(8), mode.append(RELAYOUT_HEAD)
    src = jnp.asarray(src, jnp.int32)
    mode = jnp.asarray(mode, jnp.int32)
    blk_spec = lambda off: pl.BlockSpec(
        (None, D_MODEL, BRANCH_WIDTH), lambda li, b, src, mode, off=off: (li, 0, src[b] + off))
    return pl.pallas_call(
        _relayout_kernel,
        out_shape=jax.ShapeDtypeStruct((depth, D_MODEL, Z_COLS), BF16),
        grid_spec=pltpu.PrefetchScalarGridSpec(
            num_scalar_prefetch=2,
            grid=(depth, n_blocks),
            in_specs=[blk_spec(0), blk_spec(1)],
            out_specs=pl.BlockSpec((None, D_MODEL, BRANCH_WIDTH),
                                   lambda li, b, src, mode: (li, 0, b)),
        ),
        compiler_params=_cparams(("parallel", "parallel")),
        name="relayout_w_in",
    )(src, mode, w_in, w_in)


def kernel(x, p, norm_mix, w_in, conv_w, conv_b, fgate_bias, q_norm, k_norm, lb_logits,
           hgrn_norm, sgu_norm, spatial_w, spatial_b, w_up, merge_b, w_o, norm_ple,
           w_ple_gate, w_ple_proj):
    bn, s, _ = x.shape
    depth = w_in.shape[0]
    t = bn * s
    assert w_in.shape[-1] == 15 * BRANCH_WIDTH + N_HEADS + N_BRANCH * D_MODEL
    assert s % SPATIAL_CHUNK == 0 and s % GLA_CHUNK == 0

    lb_p = jax.nn.softmax(lb_logits.astype(F32), axis=0)
    lower_bounds = jnp.clip(jnp.cumsum(lb_p, axis=0) - lb_p[0], 0.0, 1.0)
    w_z = _reorder_w_in(w_in)
    fbias = jnp.pad(fgate_bias.astype(F32), ((0, 0), (0, AUG - N_HEADS)))
    gq = jnp.tile(q_norm.astype(F32) * (HEAD_DIM ** -0.5 * LOG2E), (1, N_HEADS))
    gk = jnp.tile(k_norm.astype(F32), (1, N_HEADS))
    qk_bound = (HEAD_DIM * BF16_NORM_MARGIN) * jnp.max(jnp.abs(gq), axis=1) * jnp.max(jnp.abs(gk), axis=1)
    skip_slack = 2.0 * qk_bound + ATTN_CUTOFF_LOG2
    causal = jnp.tril(jnp.ones((SPATIAL_CHUNK, SPATIAL_CHUNK), F32))
    ws = (spatial_w.astype(F32) * causal).transpose(0, 2, 1, 3).reshape(
        depth, SPATIAL_CHUNK, N_HEADS * SPATIAL_CHUNK).astype(BF16)
    bs = jnp.repeat(spatial_b.astype(F32).transpose(0, 2, 1), HEAD_DIM, axis=-1)
    mb_cols = jnp.pad(merge_b.astype(F32).reshape(depth, 1, N_BRANCH * D_MODEL),
                      ((0, 0), (0, 0), (COL_MERGE * BRANCH_WIDTH,
                                        Z_COLS - N_BRANCH * D_MODEL - COL_MERGE * BRANCH_WIDTH)))
    w_up_b = w_up.astype(BF16)
    w_o_b = w_o.astype(BF16)
    w_g_b = w_ple_gate.astype(BF16)
    w_p_b = w_ple_proj.astype(BF16)

    xf = x.reshape(t, D_MODEL)
    p_rows = p.reshape(depth, t, PLE_DIM)
    for li in range(depth):
        z = _in_proj(xf, norm_mix[li][None], w_z, mb_cols[li], li)
        z3 = z.reshape(bn, s, Z_COLS)
        qa, ka, va, edge, o_c = _seq_mixers(z3, fbias[li][None], gq[li][None], gk[li][None],
                                            lower_bounds[li][None], hgrn_norm[li][None])
        cfirst = edge[:, :, 0, :N_HEADS].transpose(0, 2, 1).reshape(-1) + skip_slack[li]
        clast = edge[:, :, 1, :N_HEADS].transpose(0, 2, 1).reshape(-1)
        o_b = _fox_attn(qa, ka, va, cfirst, clast).reshape(t, BRANCH_WIDTH)
        xf = _merge(z, o_b, o_c.reshape(t, BRANCH_WIDTH), xf, p_rows, s, li,
                    conv_w[li], conv_b[li][None], sgu_norm[li][None], ws, bs,
                    w_up_b, w_o_b, norm_ple[li][None], w_g_b, w_p_b)
    return xf.reshape(bn, s, D_MODEL).astype(x.dtype)
```

```python
import functools
import math

import jax
import jax.numpy as jnp
import numpy as np
from jax import lax
from jax.experimental import pallas as pl
from jax.experimental.pallas import tpu as pltpu

F32 = jnp.float32
BF16 = jnp.bfloat16

D_MODEL = 1024
PLE_DIM = 256
N_BRANCH = 4
BRANCH_WIDTH = 256
HEAD_DIM = 64
N_HEADS = BRANCH_WIDTH // HEAD_DIM
CONV_WIDTH = 3
GLA_CHUNK = 64
SPATIAL_CHUNK = 128
EPS = 1e-6
MASK_VALUE = -1e30
LOG2E = math.log2(math.e)

Z_COLS = 8192
COL_MERGE = 0
COL_A = 16
COL_B = 20
COL_C = 24
COL_D = 28
COL_F = 31
SILU_COLS = (COL_A + 3, COL_B + 3, COL_C, COL_C + 3, COL_D + 2)

AUG = 128
LANE_C = HEAD_DIM
IN_PROJ_ROWS = 1024
IN_PROJ_COLS = 2048
MERGE_ROWS = 512
ATTN_BLOCK = 512
ATTN_STREAMS = 4
ATTN_WIDES = (3, 2, 1)
ATTN_CUTOFF_LOG2 = 152.0
BF16_NORM_MARGIN = 1.02

VMEM_LIMIT_BYTES = 56 * 1024 * 1024
HGRN_SAFE_RANGE = 60.0
HGRN_LOOKAHEAD = 2


def _attn_block(seq_len):
    return min(ATTN_BLOCK, seq_len)


def _cparams(semantics):
    return pltpu.CompilerParams(dimension_semantics=semantics, vmem_limit_bytes=VMEM_LIMIT_BYTES)


def _split3(x):
    a = x.astype(BF16)
    r = x - a.astype(F32)
    b = r.astype(BF16)
    c = (r - b.astype(F32)).astype(BF16)
    return a, b, c


def _pack3(x):
    a, b, c = (piece.astype(F32) for piece in _split3(x))
    lane = _iota(x.shape, 1)
    packed = jnp.where(
        lane < N_HEADS, a,
        jnp.where(lane < 2 * N_HEADS, pltpu.roll(b, N_HEADS, axis=1),
                  jnp.where(lane < 3 * N_HEADS, pltpu.roll(c, 2 * N_HEADS, axis=1), 0.0)))
    return packed.astype(BF16)


def _dot(a, b):
    return jnp.dot(a, b, preferred_element_type=F32)


def _dot_nt(a, b):
    return lax.dot_general(a, b, (((1,), (1,)), ((), ())), preferred_element_type=F32)


def _dot_tn(a, b):
    return lax.dot_general(a, b, (((0,), (0,)), ((), ())), preferred_element_type=F32)


def _iota(shape, dim):
    return lax.broadcasted_iota(jnp.int32, shape, dim)


def _head_block_mask(rows, cols):
    return (_iota((rows, cols), 0) // HEAD_DIM) == (_iota((rows, cols), 1) // HEAD_DIM)


def _group_mean_sq(x):
    w = x.shape[-1]
    avg = jnp.where(_head_block_mask(w, w), 1.0 / HEAD_DIM, 0.0).astype(BF16)
    sq = x * x
    hi = sq.astype(BF16)
    lo = (sq - hi.astype(F32)).astype(BF16)
    return _dot(hi, avg) + _dot(lo, avg)


def _in_proj_kernel(x_ref, g_ref, w_ref, mb_ref, z_ref, h_ref):
    j = pl.program_id(1)
    tn = z_ref.shape[1]
    blocks_per_step = tn // BRANCH_WIDTH
    n_steps = Z_COLS // tn

    @pl.when(j == 0)
    def _():
        x = x_ref[...]
        ms = jnp.mean(x * x, axis=-1, keepdims=True)
        h_ref[...] = (x * lax.rsqrt(ms + EPS) * g_ref[...]).astype(BF16)

    n_merge_steps = (COL_A - COL_MERGE) // blocks_per_step

    @pl.when(j < n_merge_steps)
    def _():
        res = _dot(h_ref[...], w_ref[...])
        half_logit = (0.5 * (res + mb_ref[...])).astype(z_ref.dtype)
        z_ref[...] = 0.5 * jnp.tanh(half_logit) + 0.5

    for step in range(n_merge_steps, n_steps):
        @pl.when(j == step)
        def _(step=step):
            res = _dot(h_ref[...], w_ref[...])
            for blk in range(blocks_per_step):
                piece = res[:, blk * BRANCH_WIDTH:(blk + 1) * BRANCH_WIDTH]
                if step * blocks_per_step + blk in SILU_COLS:
                    piece = jax.nn.silu(piece)
                z_ref[:, blk * BRANCH_WIDTH:(blk + 1) * BRANCH_WIDTH] = piece.astype(z_ref.dtype)


def _in_proj(x, g, w_all, mb, li):
    t = x.shape[0]
    tm = min(IN_PROJ_ROWS, t)
    tn = IN_PROJ_COLS
    assert (COL_A - COL_MERGE) * BRANCH_WIDTH % tn == 0
    return pl.pallas_call(
        _in_proj_kernel,
        out_shape=jax.ShapeDtypeStruct((t, Z_COLS), BF16),
        grid=(t // tm, Z_COLS // tn),
        in_specs=[
            pl.BlockSpec((tm, D_MODEL), lambda i, j: (i, 0)),
            pl.BlockSpec((1, D_MODEL), lambda i, j: (0, 0)),
            pl.BlockSpec((None, D_MODEL, tn), lambda i, j: (li, 0, j)),
            pl.BlockSpec((1, tn), lambda i, j: (0, j)),
        ],
        out_specs=pl.BlockSpec((tm, tn), lambda i, j: (i, j)),
        scratch_shapes=[pltpu.VMEM((tm, D_MODEL), BF16)],
        compiler_params=_cparams(("parallel", "arbitrary")),
        name="in_proj",
    )(x, g, w_all, mb)


def _attn_prep_body(q_ref, k_ref, v_ref, f_ref, fb_ref, gq_ref, gk_ref,
                    wq_ref, wk_ref, wv_ref, oq_ref, ok_ref, ov_ref,
                    qa_ref, ka_ref, va_ref, edge_ref, carry_ref):
    ts = q_ref.shape[1]

    f = f_ref[0][:, :AUG].astype(F32) + fb_ref[...]
    log_f = (jnp.minimum(f, 0.0) - jnp.log1p(jnp.exp(-jnp.abs(f)))) * LOG2E
    tril = (_iota((ts, ts), 0) >= _iota((ts, ts), 1)).astype(BF16)
    part = _dot(tril, _pack3(log_f))
    total = (part + pltpu.roll(part, AUG - N_HEADS, axis=1)
             + pltpu.roll(part, AUG - 2 * N_HEADS, axis=1))
    cum = jnp.where(_iota((ts, AUG), 1) < N_HEADS, total, 0.0) + carry_ref[...]
    carry_ref[...] = cum[ts - 1:ts, :]
    edge_row = _iota((8, AUG), 0)
    edge_ref[0, 0] = jnp.where(edge_row == 0, cum[0:1, :],
                               jnp.where(edge_row == 1, cum[ts - 1:ts, :], 0.0))

    def normed(ref, gain_ref):
        x = ref[0].astype(F32)
        return (x * lax.rsqrt(_group_mean_sq(x) + EPS) * gain_ref[...]).astype(BF16)

    ccat = _pack3(cum)
    qa = _dot(jnp.concatenate([normed(q_ref, gq_ref), ccat], axis=1), wq_ref[...]) + oq_ref[...]
    ka = _dot(jnp.concatenate([normed(k_ref, gk_ref), ccat], axis=1), wk_ref[...]) + ok_ref[...]
    va = _dot(v_ref[0], wv_ref[...]) + ov_ref[...]
    for h in range(N_HEADS):
        lanes = slice(h * AUG, (h + 1) * AUG)
        qa_ref[0, h] = qa[:, lanes].astype(BF16)
        ka_ref[0, h] = ka[:, lanes].astype(BF16)
        va_ref[0, h] = va[:, lanes].astype(BF16)


def _attn_layout_constants():
    w, wide = BRANCH_WIDTH, N_HEADS * AUG
    place = np.zeros((w, wide), np.float32)
    sel_q = np.zeros((AUG, wide), np.float32)
    sel_k = np.zeros((AUG, wide), np.float32)
    ones_q = np.zeros((1, wide), np.float32)
    ones_k = np.zeros((1, wide), np.float32)
    ones_v = np.zeros((1, wide), np.float32)
    for h in range(N_HEADS):
        for d in range(HEAD_DIM):
            place[h * HEAD_DIM + d, h * AUG + d] = 1.0
        for i in range(3):
            sel_q[i * N_HEADS + h, h * AUG + LANE_C + i] = 1.0
            sel_k[i * N_HEADS + h, h * AUG + LANE_C + 3 + i] = -1.0
            ones_k[0, h * AUG + LANE_C + i] = 1.0
            ones_q[0, h * AUG + LANE_C + 3 + i] = 1.0
        ones_v[0, h * AUG + HEAD_DIM] = 1.0
    bf = lambda a: jnp.asarray(a, BF16)
    return (bf(np.concatenate([place, sel_q])), bf(np.concatenate([place, sel_k])), bf(place),
            jnp.asarray(ones_q), jnp.asarray(ones_k), jnp.asarray(ones_v))


def _fox_attn_kernel(cfirst_ref, clast_ref, qa_ref, ka_ref, va_ref, o_ref, m_ref, acc_ref):
    bi = pl.program_id(0)
    qi = pl.program_id(1)
    nblk = pl.num_programs(1)
    tq = qa_ref.shape[2]
    tr = tq // ATTN_STREAMS

    def process(h, kj, nblocks, diagonal):
        start = pl.multiple_of(kj * tq, tq)
        widths = [(nblocks - 1) * tq + (r + 1) * tr if diagonal else nblocks * tq
                  for r in range(ATTN_STREAMS)]
        scores = [
            _dot_nt(qa_ref[0, h, r * tr:(r + 1) * tr, :],
                    ka_ref[0, h, pl.ds(start, widths[r]), :])
            for r in range(ATTN_STREAMS)]
        for r, s in enumerate(scores):
            wk = widths[r]
            if diagonal:
                edge = s[:, wk - tr:]
                edge = jnp.where(_iota((tr, tr), 1) <= _iota((tr, tr), 0), edge, MASK_VALUE)
                s = edge if wk == tr else jnp.concatenate([s[:, :wk - tr], edge], axis=1)
            row_max = jnp.max(s, axis=-1, keepdims=True)
            if diagonal:
                m_new = jnp.broadcast_to(row_max, (tr, AUG))
            else:
                m_prev = m_ref[h, r]
                m_new = jnp.maximum(m_prev, row_max)
            p = jnp.concatenate(
                [jnp.exp2(s[:, c * AUG:(c + 1) * AUG] - m_new) for c in range(wk // AUG)], axis=1)
            pv = _dot(p.astype(BF16), va_ref[0, h, pl.ds(start, wk), :])
            acc_ref[h, r] = pv if diagonal else acc_ref[h, r] * jnp.exp2(m_prev - m_new) + pv
            m_ref[h, r] = m_new

    @pl.when(qi > 0)
    def _():
        for h in range(N_HEADS):
            process(h, qi - 1, 2, True)

    @pl.when(qi == 0)
    def _():
        for h in range(N_HEADS):
            process(h, qi, 1, True)

    last_off = jnp.maximum(qi - 1, 0)
    for h in range(N_HEADS):
        base = (bi * N_HEADS + h) * nblk
        c_i = cfirst_ref[base + qi]
        j_lo = lax.while_loop(
            lambda j: (j > 0) & (c_i - clast_ref[base + jnp.maximum(j - 1, 0)] >= 0.0),
            lambda j: j - 1, last_off)
        first = j_lo
        for width in ATTN_WIDES:
            n_steps = (last_off - first) // width

            def body(t, carry, h=h, first=first, width=width):
                process(h, first + t * width, width, False)
                return carry

            lax.fori_loop(0, n_steps, body, 0)
            first = first + n_steps * width

    outs = []
    for h in range(N_HEADS):
        acc = acc_ref[h].reshape(tq, AUG)
        outs.append(acc[:, :HEAD_DIM] / acc[:, HEAD_DIM:HEAD_DIM + 1])
    o_ref[0] = jnp.concatenate(outs, axis=-1)


def _fox_attn(qa, ka, va, cfirst, clast):
    b, _, s, _ = qa.shape
    tq = _attn_block(s)
    q_spec = pl.BlockSpec((1, N_HEADS, tq, AUG), lambda bi, i, cf, cl: (bi, 0, i, 0))
    kv_spec = pl.BlockSpec((1, N_HEADS, s, AUG), lambda bi, i, cf, cl: (bi, 0, 0, 0))
    return pl.pallas_call(
        _fox_attn_kernel,
        out_shape=jax.ShapeDtypeStruct((b, s, BRANCH_WIDTH), F32),
        grid_spec=pltpu.PrefetchScalarGridSpec(
            num_scalar_prefetch=2,
            grid=(b, s // tq),
            in_specs=[q_spec, kv_spec, kv_spec],
            out_specs=pl.BlockSpec((1, tq, BRANCH_WIDTH), lambda bi, i, cf, cl: (bi, i, 0)),
            scratch_shapes=[pltpu.VMEM((N_HEADS, ATTN_STREAMS, tq // ATTN_STREAMS, AUG), F32),
                            pltpu.VMEM((N_HEADS, ATTN_STREAMS, tq // ATTN_STREAMS, AUG), F32)],
        ),
        compiler_params=_cparams(("parallel", "arbitrary")),
        name="fox_attn",
    )(cfirst, clast, qa, ka, va)


def _hgrn_body(q_ref, f_ref, i_ref, lb_ref, gain_ref, o_ref,
               state_ref, qs_ref, ks_ref, vs_ref, lg_ref, oc_ref):
    tc = q_ref.shape[1]
    c = GLA_CHUNK
    half = c // 2
    w = BRANCH_WIDTH
    n = tc // c
    lb = lb_ref[...]
    bd_mask = _head_block_mask(w, w)

    q = q_ref[0].astype(F32)
    fl = f_ref[0].astype(F32)
    v = i_ref[0].astype(F32)
    t_small = jnp.exp(-jnp.abs(fl))
    s_big = 1.0 / (1.0 + t_small)
    s_small = t_small * s_big
    pos = fl >= 0.0
    log_g = jnp.log(lb + (1.0 - lb) * jnp.where(pos, s_big, s_small))
    kf = (1.0 - lb) * jnp.where(pos, s_small, s_big)

    r_i, c_i = _iota((tc, tc), 0), _iota((tc, tc), 1)
    tril = ((r_i >= c_i) & (r_i // c == c_i // c)).astype(BF16)
    l1, l2, l3 = _split3(log_g)
    bsum = _dot(tril, l1) + _dot(tril, l2) + _dot(tril, l3)

    b3 = bsum.reshape(n, c, w)
    b_first, b_q1 = b3[:, 0:1], b3[:, half // 2 - 1:half // 2]
    b_mid, b_mid1 = b3[:, half - 1:half], b3[:, half:half + 1]
    b_q3, b_last = b3[:, half + half // 2 - 1:half + half // 2], b3[:, c - 1:c]
    spread = jnp.maximum(jnp.maximum(b_first - b_q1, b_q1 - b_mid),
                         jnp.maximum(b_mid1 - b_q3, b_q3 - b_last))
    safe = jnp.max(spread) <= HGRN_SAFE_RANGE

    @pl.when(safe)
    def _():
        top = _iota((n, c, w), 1) < half
        ref_d = jnp.where(top, b_q1, b_q3)
        q3, k3 = q.reshape(n, c, w), kf.reshape(n, c, w)
        qd = q3 * jnp.exp(b3 - ref_d)
        kd = k3 * jnp.exp(ref_d - b3)
        qo = jnp.where(top, 0.0, q3 * jnp.exp(jnp.minimum(b3 - b_mid, 0.0)))
        ko = jnp.where(top, k3 * jnp.exp(jnp.minimum(b_mid - b3, 0.0)), 0.0)
        qcat = jnp.concatenate(
            [qo, jnp.where(top, qd, 0.0), jnp.where(top, 0.0, qd)], axis=2).astype(BF16)
        kparts = [x.astype(BF16) for x in (ko, jnp.where(top, kd, 0.0), jnp.where(top, 0.0, kd))]
        q_in = (q3 * jnp.exp(b3)).astype(BF16)
        k_end = (k3 * jnp.exp(b_last - b3)).astype(BF16)
        e_last = jnp.exp(b_last)
        v_b = v.astype(BF16).reshape(n, c, w)
        causal = (_iota((c, w), 1) % HEAD_DIM) <= _iota((c, w), 0)

        def block_diag(x):
            return jnp.where(bd_mask, jnp.concatenate([x] * N_HEADS, axis=0), 0.0)

        def score(ci):
            kcat = jnp.concatenate([block_diag(kp[ci]) for kp in kparts], axis=1)
            return jnp.where(causal, _dot_nt(qcat[ci], kcat), 0.0).astype(BF16)

        def update(ci):
            return jnp.where(bd_mask, _dot_tn(v_b[ci], k_end[ci]), 0.0)

        ahead = HGRN_LOOKAHEAD
        scores = {ci: score(ci) for ci in range(min(ahead, n))}
        updates = {ci: update(ci) for ci in range(min(ahead, n))}
        state_t = state_ref[...]
        outs = []
        for ci in range(n):
            if ci + ahead < n:
                scores[ci + ahead] = score(ci + ahead)
                updates[ci + ahead] = update(ci + ahead)
            o_intra = _dot(scores.pop(ci), block_diag(v_b[ci]))
            outs.append(o_intra + _dot_nt(q_in[ci], state_t.astype(BF16)))
            state_t = state_t * e_last[ci] + updates.pop(ci)
        state_ref[...] = state_t
        oc_ref[...] = jnp.concatenate(outs, axis=0)

    @pl.when(jnp.logical_not(safe))
    def _():
        qs_ref[...] = q
        ks_ref[...] = kf
        vs_ref[...] = v
        lg_ref[...] = log_g

        def body(t, carry):
            g_t = jnp.exp(lg_ref[pl.ds(t, 1), :])
            outer = _dot_tn(vs_ref[pl.ds(t, 1), :].astype(BF16),
                            ks_ref[pl.ds(t, 1), :].astype(BF16))
            st = state_ref[...] * g_t + jnp.where(bd_mask, outer, 0.0)
            state_ref[...] = st
            oc_ref[pl.ds(t, 1), :] = _dot_nt(qs_ref[pl.ds(t, 1), :].astype(BF16), st.astype(BF16))
            return carry

        lax.fori_loop(0, tc, body, 0)

    o = oc_ref[...]
    o_ref[0] = o * lax.rsqrt(_group_mean_sq(o) + EPS) * gain_ref[...]


N_PREP_IN = 13
N_HGRN_IN = 5


def _seq_mixers_kernel(*refs):
    prep_in = refs[:N_PREP_IN]
    hgrn_in = refs[N_PREP_IN:N_PREP_IN + N_HGRN_IN]
    qa_ref, ka_ref, va_ref, edge_ref, o_ref = refs[N_PREP_IN + N_HGRN_IN:N_PREP_IN + N_HGRN_IN + 5]
    carry_ref, state_ref, qs_ref, ks_ref, vs_ref, lg_ref, oc_ref = refs[N_PREP_IN + N_HGRN_IN + 5:]

    @pl.when(pl.program_id(1) == 0)
    def _():
        carry_ref[...] = jnp.zeros_like(carry_ref)
        state_ref[...] = jnp.zeros_like(state_ref)

    _attn_prep_body(*prep_in, qa_ref, ka_ref, va_ref, edge_ref, carry_ref)
    _hgrn_body(*hgrn_in, o_ref, state_ref, qs_ref, ks_ref, vs_ref, lg_ref, oc_ref)


def _seq_mixers(z3, fbias, gq, gk, lb, gain):
    b, s, _ = z3.shape
    ts = _attn_block(s)
    col = lambda c: pl.BlockSpec((1, ts, BRANCH_WIDTH), lambda bi, si, c=c: (bi, si, c))
    full = lambda a: pl.BlockSpec(a.shape, lambda bi, si: (0, 0))
    aug = pl.BlockSpec((1, N_HEADS, ts, AUG), lambda bi, si: (bi, 0, si, 0))
    aug_shape = jax.ShapeDtypeStruct((b, N_HEADS, s, AUG), BF16)
    edge_shape = jax.ShapeDtypeStruct((b, s // ts, 8, AUG), F32)
    prep_consts = (fbias, gq, gk) + _attn_layout_constants()
    prep_specs = [col(COL_B), col(COL_B + 1), col(COL_B + 2), col(COL_F)] + [full(a) for a in prep_consts]
    hgrn_specs = [col(COL_C), col(COL_C + 1), col(COL_C + 2), full(lb), full(gain)]
    assert len(prep_specs) == N_PREP_IN and len(hgrn_specs) == N_HGRN_IN
    tile = pltpu.VMEM((ts, BRANCH_WIDTH), F32)
    return pl.pallas_call(
        _seq_mixers_kernel,
        out_shape=(aug_shape, aug_shape, aug_shape, edge_shape,
                   jax.ShapeDtypeStruct((b, s, BRANCH_WIDTH), F32)),
        grid=(b, s // ts),
        in_specs=prep_specs + hgrn_specs,
        out_specs=(aug, aug, aug, pl.BlockSpec((1, 1, 8, AUG), lambda bi, si: (bi, si, 0, 0)),
                   pl.BlockSpec((1, ts, BRANCH_WIDTH), lambda bi, si: (bi, si, 0))),
        scratch_shapes=[pltpu.VMEM((1, AUG), F32),
                        pltpu.VMEM((BRANCH_WIDTH, BRANCH_WIDTH), F32),
                        tile, tile, tile, tile, tile],
        compiler_params=_cparams(("parallel", "arbitrary")),
        name="seq_mixers",
    )(z3, z3, z3, z3, *prep_consts, z3, z3, z3, lb, gain)


HALO = 16


def _merge_kernel(zm_ref, za_ref, du_ref, dv_ref, dg_ref, bg_ref, cg_ref, hx_ref, hc_ref, ob_ref, oc_ref,
                  x_ref, p_ref, cw_ref, cb_ref, gv_ref, ws_ref, bs_ref, wup_ref,
                  wo_ref, gp_ref, wg_ref, wp_ref, out_ref, *, tiles_per_seq):
    tm = x_ref.shape[0]
    w = BRANCH_WIDTH

    za = za_ref[...].astype(F32)
    zc = za[:, 2 * w:3 * w] * za[:, 0:w]
    halo = hc_ref[...].astype(F32) * hx_ref[...].astype(F32)
    halo = jnp.where(pl.program_id(0) % tiles_per_seq == 0, 0.0, halo)
    ext = jnp.concatenate([halo, zc], axis=0)
    conv = zc * cw_ref[2:3, :]
    for tap in range(CONV_WIDTH - 1):
        shift = CONV_WIDTH - 1 - tap
        conv = conv + pltpu.roll(ext, shift, axis=0)[HALO:] * cw_ref[tap:tap + 1, :]
    y_a = za[:, w:2 * w] * (conv + cb_ref[...]) * za[:, 3 * w:4 * w]

    y_b = ob_ref[...] * bg_ref[...].astype(F32)
    y_c = oc_ref[...] * cg_ref[...].astype(F32)

    vd = dv_ref[...].astype(F32)
    vn = (vd * lax.rsqrt(_group_mean_sq(vd) + EPS) * gv_ref[...])
    lane_head = _iota((SPATIAL_CHUNK, w), 1) // HEAD_DIM
    mixed = []
    for ci in range(tm // SPATIAL_CHUNK):
        vc = vn[ci * SPATIAL_CHUNK:(ci + 1) * SPATIAL_CHUNK]
        stacked = jnp.concatenate(
            [jnp.where(lane_head == g, vc, 0.0) for g in range(N_HEADS)], axis=0).astype(BF16)
        mixed.append(_dot(ws_ref[...], stacked) + bs_ref[...])
    y_d = du_ref[...].astype(F32) * jnp.concatenate(mixed, axis=0) * dg_ref[...].astype(F32)

    merged = None
    for bi, y in enumerate((y_a, y_b, y_c, y_d)):
        gate = zm_ref[:, bi * D_MODEL:(bi + 1) * D_MODEL].astype(F32)
        term = gate * _dot(y.astype(BF16), wup_ref[bi])
        merged = term if merged is None else merged + term
    x1 = x_ref[...] + _dot(merged.astype(BF16), wo_ref[...])

    ms = jnp.mean(x1 * x1, axis=-1, keepdims=True)
    hp = (x1 * lax.rsqrt(ms + EPS) * gp_ref[...]).astype(BF16)
    ple = _dot(p_ref[...].astype(BF16), wp_ref[...])
    out_ref[...] = x1 + jax.nn.sigmoid(_dot(hp, wg_ref[...])) * ple


def _merge(z, o_b, o_c, x, p, seq_len, li, cw, cb, gv, ws, bs, wup, wo, gp, wg, wp):
    t = x.shape[0]
    tm = min(MERGE_ROWS, seq_len)
    w = BRANCH_WIDTH
    hstep = tm // HALO
    row = lambda width, c: pl.BlockSpec((tm, width), lambda i, c=c: (i, c))
    halo = lambda c: pl.BlockSpec((HALO, w), lambda i, c=c: (jnp.maximum(i * hstep - 1, 0), c))

    stacked = {id(a) for a in (ws, bs, wup, wo, wg, wp)}

    def full(a):
        if id(a) in stacked:
            return pl.BlockSpec((None,) + a.shape[1:], lambda i, n=a.ndim: (li,) + (0,) * (n - 1))
        return pl.BlockSpec(a.shape, lambda i, n=a.ndim: (0,) * n)

    consts = (cw, cb, gv, ws, bs, wup, wo, gp, wg, wp)
    return pl.pallas_call(
        functools.partial(_merge_kernel, tiles_per_seq=seq_len // tm),
        out_shape=jax.ShapeDtypeStruct((t, D_MODEL), F32),
        grid=(t // tm,),
        in_specs=[
            row(N_BRANCH * D_MODEL, COL_MERGE),
            row(4 * w, COL_A // 4),
            row(w, COL_D), row(w, COL_D + 1), row(w, COL_D + 2),
            row(w, COL_B + 3), row(w, COL_C + 3),
            halo(COL_A), halo(COL_A + 2),
            row(w, 0), row(w, 0),
            row(D_MODEL, 0),
            pl.BlockSpec((None, tm, PLE_DIM), lambda i: (li, i, 0)),
        ] + [full(a) for a in consts],
        out_specs=pl.BlockSpec((tm, D_MODEL), lambda i: (i, 0)),
        compiler_params=_cparams(("parallel",)),
        name="merge",
    )(z, z, z, z, z, z, z, z, z, o_b, o_c, x, p, *consts)


def _reorder_w_in(w_in):
    w = BRANCH_WIDTH
    n_pre = 8 * w
    n_f = N_HEADS
    n_cd = 7 * w
    wb = w_in.astype(BF16)
    pre = wb[:, :, :n_pre]
    fcols = wb[:, :, n_pre:n_pre + n_f]
    cd = wb[:, :, n_pre + n_f:n_pre + n_f + n_cd]
    merge = wb[:, :, n_pre + n_f + n_cd:]
    pad = jnp.zeros(w_in.shape[:2] + (w - n_f,), BF16)
    return jnp.concatenate([merge, pre, cd, fcols, pad], axis=-1)


def kernel(x, p, norm_mix, w_in, conv_w, conv_b, fgate_bias, q_norm, k_norm, lb_logits,
           hgrn_norm, sgu_norm, spatial_w, spatial_b, w_up, merge_b, w_o, norm_ple,
           w_ple_gate, w_ple_proj):
    bn, s, _ = x.shape
    depth = w_in.shape[0]
    t = bn * s
    assert w_in.shape[-1] == 15 * BRANCH_WIDTH + N_HEADS + N_BRANCH * D_MODEL
    assert s % SPATIAL_CHUNK == 0 and s % GLA_CHUNK == 0

    lb_p = jax.nn.softmax(lb_logits.astype(F32), axis=0)
    lower_bounds = jnp.clip(jnp.cumsum(lb_p, axis=0) - lb_p[0], 0.0, 1.0)
    w_z = _reorder_w_in(w_in)
    fbias = jnp.pad(fgate_bias.astype(F32), ((0, 0), (0, AUG - N_HEADS)))
    gq = jnp.tile(q_norm.astype(F32) * (HEAD_DIM ** -0.5 * LOG2E), (1, N_HEADS))
    gk = jnp.tile(k_norm.astype(F32), (1, N_HEADS))
    qk_bound = (HEAD_DIM * BF16_NORM_MARGIN) * jnp.max(jnp.abs(gq), axis=1) * jnp.max(jnp.abs(gk), axis=1)
    skip_slack = 2.0 * qk_bound + ATTN_CUTOFF_LOG2
    causal = jnp.tril(jnp.ones((SPATIAL_CHUNK, SPATIAL_CHUNK), F32))
    ws = (spatial_w.astype(F32) * causal).transpose(0, 2, 1, 3).reshape(
        depth, SPATIAL_CHUNK, N_HEADS * SPATIAL_CHUNK).astype(BF16)
    bs = jnp.repeat(spatial_b.astype(F32).transpose(0, 2, 1), HEAD_DIM, axis=-1)
    mb_cols = jnp.pad(merge_b.astype(F32).reshape(depth, 1, N_BRANCH * D_MODEL),
                      ((0, 0), (0, 0), (COL_MERGE * BRANCH_WIDTH,
                                        Z_COLS - N_BRANCH * D_MODEL - COL_MERGE * BRANCH_WIDTH)))
    w_up_b = w_up.astype(BF16)
    w_o_b = w_o.astype(BF16)
    w_g_b = w_ple_gate.astype(BF16)
    w_p_b = w_ple_proj.astype(BF16)

    xf = x.reshape(t, D_MODEL)
    p_rows = p.reshape(depth, t, PLE_DIM)
    for li in range(depth):
        z = _in_proj(xf, norm_mix[li][None], w_z, mb_cols[li], li)
        z3 = z.reshape(bn, s, Z_COLS)
        qa, ka, va, edge, o_c = _seq_mixers(z3, fbias[li][None], gq[li][None], gk[li][None],
                                            lower_bounds[li][None], hgrn_norm[li][None])
        cfirst = edge[:, :, 0, :N_HEADS].transpose(0, 2, 1).reshape(-1) + skip_slack[li]
        clast = edge[:, :, 1, :N_HEADS].transpose(0, 2, 1).reshape(-1)
        o_b = _fox_attn(qa, ka, va, cfirst, clast).reshape(t, BRANCH_WIDTH)
        xf = _merge(z, o_b, o_c.reshape(t, BRANCH_WIDTH), xf, p_rows, s, li,
                    conv_w[li], conv_b[li][None], sgu_norm[li][None], ws, bs,
                    w_up_b, w_o_b, norm_ple[li][None], w_g_b, w_p_b)
    return xf.reshape(bn, s, D_MODEL).astype(x.dtype)
```
